```python
import math
import jax, jax.numpy as jnp
from jax import lax
import numpy as np

D_MODEL = 2048
BATCH = 8
SEQ = 2048
DEPTH = 1

MIX_WIDTH = D_MODEL
ATTN_WIDTH = MIX_WIDTH // 2
DELTA_WIDTH = MIX_WIDTH - ATTN_WIDTH

ATTN_HEAD_DIM = 64
N_ATTN_HEADS = ATTN_WIDTH // ATTN_HEAD_DIM
N_KV_HEADS = N_ATTN_HEADS // 4
GQA_GROUP = N_ATTN_HEADS // N_KV_HEADS
WINDOW = 128
ATTN_BLOCK = 128
NEG_INF = -1e30

N_BUCKETS = 32
MAX_DISTANCE = 128

DELTA_HEAD_DIM = 128
N_DELTA_HEADS = DELTA_WIDTH // DELTA_HEAD_DIM
CONV_WIDTH = 4
CHUNK = 64

D_FF = 4 * D_MODEL

DN_ALPHA = (2.0 * DEPTH) ** 0.25
DN_BETA = (8.0 * DEPTH) ** -0.25
LN_EPS = 1e-5
RMS_EPS = 1e-6

COL_ATTN_Q = N_ATTN_HEADS * ATTN_HEAD_DIM
COL_ATTN_KV = N_KV_HEADS * ATTN_HEAD_DIM
COL_DELTA_QKV = 3 * DELTA_WIDTH
COL_DELTA_SCALAR = N_DELTA_HEADS
COL_DELTA_Z = DELTA_WIDTH
N_IN_COLS = COL_ATTN_Q + 2 * COL_ATTN_KV + COL_DELTA_QKV + 2 * COL_DELTA_SCALAR + COL_DELTA_Z
SPLIT_POINTS = [int(s) for s in np.cumsum([COL_ATTN_Q, COL_ATTN_KV, COL_ATTN_KV, COL_DELTA_QKV,
                                              COL_DELTA_SCALAR, COL_DELTA_SCALAR])]

kernel_name = "hymba_swa_sink_gdn_deepnorm"


def layer_norm(x, g, b):
    xf = x.astype(jnp.float32)
    mu = jnp.mean(xf, axis=-1, keepdims=True)
    xc = xf - mu
    var = jnp.mean(xc * xc, axis=-1, keepdims=True)
    y = xc * lax.rsqrt(var + LN_EPS) * g.astype(jnp.float32) + b.astype(jnp.float32)
    return y.astype(x.dtype)


def t5_causal_bucket(dist):
    n = jnp.maximum(dist, 0)
    max_exact = N_BUCKETS // 2
    nf = jnp.maximum(n, 1).astype(jnp.float32)
    large = max_exact + (jnp.log(nf / max_exact) / math.log(MAX_DISTANCE / max_exact)
                         * (N_BUCKETS - max_exact)).astype(jnp.int32)
    large = jnp.minimum(large, N_BUCKETS - 1)
    return jnp.where(n < max_exact, n, large)


def sliding_window_attention(q, k, v, sinks, rel_bias):
    B, S = q.shape[0], q.shape[1]
    nb = S // ATTN_BLOCK
    qb = q.reshape(B, nb, ATTN_BLOCK, N_KV_HEADS, GQA_GROUP, ATTN_HEAD_DIM)
    kb = k.reshape(B, nb, ATTN_BLOCK, N_KV_HEADS, ATTN_HEAD_DIM)
    vb = v.reshape(B, nb, ATTN_BLOCK, N_KV_HEADS, ATTN_HEAD_DIM)
    pad = ((0, 0), (1, 0), (0, 0), (0, 0), (0, 0))
    kc = jnp.concatenate([jnp.pad(kb, pad)[:, :-1], kb], axis=2)
    vc = jnp.concatenate([jnp.pad(vb, pad)[:, :-1], vb], axis=2)
    logits = jnp.einsum('bnqhgd,bnkhd->bnhgqk', qb, kc).astype(jnp.float32) * (ATTN_HEAD_DIM ** -0.5)
    qi = jnp.arange(ATTN_BLOCK, dtype=jnp.int32)[:, None]
    kj = jnp.arange(2 * ATTN_BLOCK, dtype=jnp.int32)[None, :]
    dist = qi + ATTN_BLOCK - kj
    band = (dist >= 0) & (dist < WINDOW)
    blk = jnp.arange(nb, dtype=jnp.int32)[:, None, None]
    valid = band[None] & ~((blk == 0) & (kj[None] < ATTN_BLOCK))
    bias = rel_bias.astype(jnp.float32)[t5_causal_bucket(dist)]
    bias = bias.transpose(2, 0, 1).reshape(N_KV_HEADS, GQA_GROUP, ATTN_BLOCK, 2 * ATTN_BLOCK)
    logits = jnp.where(valid[None, :, None, None], logits + bias, NEG_INF)
    sink = jnp.broadcast_to(sinks.astype(jnp.float32).reshape(1, 1, N_KV_HEADS, GQA_GROUP, 1, 1),
                            logits.shape[:-1] + (1,))
    probs = jax.nn.softmax(jnp.concatenate([logits, sink], axis=-1), axis=-1)[..., :-1]
    out = jnp.einsum('bnhgqk,bnkhd->bnqhgd', probs.astype(v.dtype), vc)
    return out.reshape(B, S, N_ATTN_HEADS * ATTN_HEAD_DIM)


def chunked_gated_delta_rule(q, k, v, g, beta):
    B, S, H, dk = q.shape
    dv = v.shape[-1]
    nc = S // CHUNK

    def chunkify(t):
        return t.reshape(B, nc, CHUNK, H, -1).transpose(0, 3, 1, 2, 4)

    qc, kc, vc = chunkify(q), chunkify(k), chunkify(v)
    gc = g.reshape(B, nc, CHUNK, H).transpose(0, 3, 1, 2)
    bc = beta.reshape(B, nc, CHUNK, H).transpose(0, 3, 1, 2)
    G = jnp.cumsum(gc, axis=-1)
    tril = jnp.tril(jnp.ones((CHUNK, CHUNK), dtype=bool))
    strict = jnp.tril(jnp.ones((CHUNK, CHUNK), dtype=bool), -1)
    decay_mat = jnp.exp(jnp.where(tril, G[..., :, None] - G[..., None, :], -jnp.inf))
    kbeta = kc * bc[..., None]
    A = jnp.where(strict, jnp.einsum('bhnid,bhnjd->bhnij', kbeta, kc) * decay_mat, 0.0)
    eye = jnp.eye(CHUNK, dtype=jnp.float32)
    rhs = jnp.concatenate([vc * bc[..., None], kbeta * jnp.exp(G)[..., None]], axis=-1)
    sol = lax.linalg.triangular_solve(A + eye, rhs, left_side=True, lower=True, unit_diagonal=True)
    u, w = sol[..., :dv], sol[..., dv:]
    attn_intra = jnp.einsum('bhnid,bhnjd->bhnij', qc, kc) * decay_mat
    q_dec = qc * jnp.exp(G)[..., None]
    k_dec = kc * jnp.exp(G[..., -1:] - G)[..., None]
    g_last = jnp.exp(G[..., -1])

    def step(state, inp):
        q_d, k_d, u_c, w_c, a_c, gl = inp
        v_new = u_c - jnp.einsum('bhcd,bhde->bhce', w_c, state)
        o = jnp.einsum('bhcd,bhde->bhce', q_d, state) + jnp.einsum('bhij,bhje->bhie', a_c, v_new)
        state = state * gl[..., None, None] + jnp.einsum('bhcd,bhce->bhde', k_d, v_new)
        return state, o

    xs = (jnp.moveaxis(q_dec, 2, 0), jnp.moveaxis(k_dec, 2, 0), jnp.moveaxis(u, 2, 0),
          jnp.moveaxis(w, 2, 0), jnp.moveaxis(attn_intra, 2, 0), jnp.moveaxis(g_last, 2, 0))
    s0 = jnp.zeros((B, H, dk, dv), dtype=jnp.float32)
    _, o = lax.scan(step, s0, xs)
    return o.transpose(1, 0, 3, 2, 4).reshape(B, S, H, dv)


def l2_normalise(t):
    return t * lax.rsqrt(jnp.sum(t * t, axis=-1, keepdims=True) + RMS_EPS)


def hybrid_mixer(h, w_in, conv_w, a_log, dt_bias, delta_norm_w, sinks, rel_bias, w_o):
    B, S, _ = h.shape
    proj = h @ w_in
    q_a, k_a, v_a, qkv_d, a_raw, b_raw, z = jnp.split(proj, SPLIT_POINTS, axis=-1)

    attn_out = sliding_window_attention(
        q_a.reshape(B, S, N_ATTN_HEADS, ATTN_HEAD_DIM),
        k_a.reshape(B, S, N_KV_HEADS, ATTN_HEAD_DIM),
        v_a.reshape(B, S, N_KV_HEADS, ATTN_HEAD_DIM), sinks, rel_bias)

    qkv_d = lax.conv_general_dilated(qkv_d, conv_w, window_strides=(1,), padding=[(CONV_WIDTH - 1, 0)],
                                     dimension_numbers=('NWC', 'WIO', 'NWC'),
                                     feature_group_count=COL_DELTA_QKV)
    qkv_d = jax.nn.silu(qkv_d).astype(jnp.float32)
    q_d, k_d, v_d = jnp.split(qkv_d, 3, axis=-1)
    q_d = l2_normalise(q_d.reshape(B, S, N_DELTA_HEADS, DELTA_HEAD_DIM)) * (DELTA_HEAD_DIM ** -0.5)
    k_d = l2_normalise(k_d.reshape(B, S, N_DELTA_HEADS, DELTA_HEAD_DIM))
    v_d = v_d.reshape(B, S, N_DELTA_HEADS, DELTA_HEAD_DIM)
    g = -jnp.exp(a_log.astype(jnp.float32)) * jax.nn.softplus(a_raw.astype(jnp.float32) + dt_bias.astype(jnp.float32))
    beta = jax.nn.sigmoid(b_raw.astype(jnp.float32))
    o_d = chunked_gated_delta_rule(q_d, k_d, v_d, g, beta)
    o_d = o_d * lax.rsqrt(jnp.mean(o_d * o_d, axis=-1, keepdims=True) + RMS_EPS) * delta_norm_w.astype(jnp.float32)
    o_d = o_d * jax.nn.silu(z.astype(jnp.float32)).reshape(B, S, N_DELTA_HEADS, DELTA_HEAD_DIM)
    delta_out = o_d.reshape(B, S, DELTA_WIDTH).astype(h.dtype)

    mix = jnp.concatenate([attn_out, delta_out], axis=-1)
    return mix @ w_o


def squared_relu_mlp(h, w_up, w_down):
    a = jax.nn.relu(h @ w_up)
    return (a * a) @ w_down


def setup_inputs(seed: int = 0) -> dict:
    key = jax.random.key(seed)
    ks = jax.random.split(key, 16)
    f32 = jnp.float32
    x = jax.random.normal(ks[0], (BATCH, SEQ, D_MODEL), f32)
    w_in = jax.random.normal(ks[1], (DEPTH, D_MODEL, N_IN_COLS), f32) * D_MODEL ** -0.5
    conv_w = jax.random.normal(ks[2], (DEPTH, CONV_WIDTH, 1, COL_DELTA_QKV), f32) * CONV_WIDTH ** -0.5
    a_log = jnp.log(jax.random.uniform(ks[3], (DEPTH, N_DELTA_HEADS), f32, 1.0, 16.0))
    dt = jnp.exp(jax.random.uniform(ks[4], (DEPTH, N_DELTA_HEADS), f32, math.log(1e-3), math.log(1e-1)))
    dt_bias = dt + jnp.log(-jnp.expm1(-dt))
    delta_norm_w = 1.0 + 0.02 * jax.random.normal(ks[5], (DEPTH, DELTA_HEAD_DIM), f32)
    attn_sinks = 0.5 * jax.random.normal(ks[6], (DEPTH, N_ATTN_HEADS), f32)
    rel_bias = 0.5 * jax.random.normal(ks[7], (N_BUCKETS, N_ATTN_HEADS), f32)
    w_o = jax.random.normal(ks[8], (DEPTH, MIX_WIDTH, D_MODEL), f32) * (MIX_WIDTH ** -0.5 * DN_BETA)
    ln1_g = 1.0 + 0.02 * jax.random.normal(ks[9], (DEPTH, D_MODEL), f32)
    ln1_b = 0.02 * jax.random.normal(ks[10], (DEPTH, D_MODEL), f32)
    w_up = jax.random.normal(ks[11], (DEPTH, D_MODEL, D_FF), f32) * D_MODEL ** -0.5
    w_down = jax.random.normal(ks[12], (DEPTH, D_FF, D_MODEL), f32) * (D_FF ** -0.5 * DN_BETA)
    ln2_g = 1.0 + 0.02 * jax.random.normal(ks[13], (DEPTH, D_MODEL), f32)
    ln2_b = 0.02 * jax.random.normal(ks[14], (DEPTH, D_MODEL), f32)
    return {"x": x, "w_in": w_in, "conv_w": conv_w, "a_log": a_log, "dt_bias": dt_bias,
            "delta_norm_w": delta_norm_w, "attn_sinks": attn_sinks, "rel_bias": rel_bias,
            "w_o": w_o, "ln1_g": ln1_g, "ln1_b": ln1_b, "w_up": w_up, "w_down": w_down,
            "ln2_g": ln2_g, "ln2_b": ln2_b}


def reference(x, w_in, conv_w, a_log, dt_bias, delta_norm_w, attn_sinks, rel_bias,
              w_o, ln1_g, ln1_b, w_up, w_down, ln2_g, ln2_b):
    for l in range(DEPTH):
        mixed = hybrid_mixer(x, w_in[l], conv_w[l], a_log[l], dt_bias[l], delta_norm_w[l],
                             attn_sinks[l], rel_bias, w_o[l])
        x = layer_norm(DN_ALPHA * x + mixed, ln1_g[l], ln1_b[l])
        x = layer_norm(DN_ALPHA * x + squared_relu_mlp(x, w_up[l], w_down[l]), ln2_g[l], ln2_b[l])
    return x
```

```python
import functools
import math

import jax
import jax.numpy as jnp
import numpy as np
from jax import lax
from jax.experimental import pallas as pl
from jax.experimental.pallas import tpu as pltpu

F32 = jnp.float32
BF16 = jnp.bfloat16

D_MODEL = 2048
ATTN_HEAD_DIM = 64
N_ATTN_HEADS = 16
N_KV_HEADS = 4
ATTN_BLOCK = 128
WINDOW = 128
NEG_INF = -1e30
N_BUCKETS = 32
MAX_DISTANCE = 128
DELTA_HEAD_DIM = 128
N_DELTA_HEADS = 8
DELTA_WIDTH = N_DELTA_HEADS * DELTA_HEAD_DIM
CONV_WIDTH = 4
CHUNK = 64
D_FF = 4 * D_MODEL
DEPTH = 1
DN_ALPHA = (2.0 * DEPTH) ** 0.25
LN_EPS = 1e-5
RMS_EPS = 1e-6

ATTN_Q = N_ATTN_HEADS * ATTN_HEAD_DIM
ATTN_KV = N_KV_HEADS * ATTN_HEAD_DIM
DELTA_QKV = 3 * DELTA_WIDTH

LANES = 128
SUBLANES = 8
VMEM_LIMIT_BYTES = 56 * 1024 * 1024

COL_DQKV = 0
COL_Z = COL_DQKV + DELTA_QKV
COL_Q = COL_Z + DELTA_WIDTH
COL_K = COL_Q + ATTN_Q
COL_V = COL_K + ATTN_KV
COL_AB = COL_V + ATTN_KV
PROJ_COLS = COL_AB + LANES


def _dot(a, b):
    return jnp.dot(a, b, preferred_element_type=F32)


def _dot_nt(a, b):
    return lax.dot_general(a, b, (((1,), (1,)), ((), ())), preferred_element_type=F32)


def _dot_tn(a, b):
    return lax.dot_general(a, b, (((0,), (0,)), ((), ())), preferred_element_type=F32)


def _layer_norm(y, g, b):
    mu = jnp.mean(y, axis=-1, keepdims=True)
    yc = y - mu
    var = jnp.mean(yc * yc, axis=-1, keepdims=True)
    return yc * lax.rsqrt(var + LN_EPS) * g + b


IN_TM = 1024
IN_TN = 640


def _in_proj_kernel(x_ref, w_ref, o_ref, xb_ref):
    @pl.when(pl.program_id(1) == 0)
    def _():
        xb_ref[...] = x_ref[...].astype(BF16)

    o_ref[...] = _dot(xb_ref[...], w_ref[...])


def _in_proj(x2, w_all):
    t = x2.shape[0]
    return pl.pallas_call(
        _in_proj_kernel,
        out_shape=jax.ShapeDtypeStruct((t, PROJ_COLS), F32),
        grid=(t // IN_TM, PROJ_COLS // IN_TN),
        in_specs=[
            pl.BlockSpec((IN_TM, D_MODEL), lambda i, j: (i, 0)),
            pl.BlockSpec((D_MODEL, IN_TN), lambda i, j: (0, j)),
        ],
        out_specs=pl.BlockSpec((IN_TM, IN_TN), lambda i, j: (i, j)),
        scratch_shapes=[pltpu.VMEM((IN_TM, D_MODEL), BF16)],
        compiler_params=pltpu.CompilerParams(
            dimension_semantics=("arbitrary", "arbitrary"),
            vmem_limit_bytes=VMEM_LIMIT_BYTES),
        name="in_proj",
    )(x2, w_all)


def _t5_bucket_table():
    qi = np.arange(ATTN_BLOCK, dtype=np.int64)[:, None]
    kj = np.arange(2 * ATTN_BLOCK, dtype=np.int64)[None, :]
    dist = qi + ATTN_BLOCK - kj
    n = np.maximum(dist, 0)
    max_exact = N_BUCKETS // 2
    nf = np.maximum(n, 1).astype(np.float64)
    large = max_exact + (np.log(nf / max_exact) / math.log(MAX_DISTANCE / max_exact)
                         * (N_BUCKETS - max_exact)).astype(np.int64)
    large = np.minimum(large, N_BUCKETS - 1)
    bucket = np.where(n < max_exact, n, large)
    band = (dist >= 0) & (dist < WINDOW)
    return np.where(band, bucket, -1).astype(np.int32)


def _attn_kernel(bucket_ref, relb_ref, sink_ref, q_ref, kp_ref, kc_ref, vp_ref, vc_ref,
                 o_ref, bias_ref):
    b = pl.program_id(0)
    n = pl.program_id(1)

    @pl.when((b == 0) & (n == 0))
    def _():
        bucket = bucket_ref[...]
        col = lax.broadcasted_iota(jnp.int32, bucket.shape, 1)

        def head_body(h, carry):
            acc = jnp.full(bucket.shape, NEG_INF, F32)
            for bk in range(N_BUCKETS):
                acc = jnp.where(bucket == bk, relb_ref[bk, h], acc)
            bias_ref[0, h] = acc
            bias_ref[1, h] = jnp.where(col < ATTN_BLOCK, NEG_INF, acc)
            return carry

        lax.fori_loop(0, N_ATTN_HEADS, head_body, 0)

    table = jnp.where(n == 0, 1, 0)
    lane = lax.broadcasted_iota(jnp.int32, (2 * ATTN_BLOCK, LANES), 1)
    lo = lane < ATTN_HEAD_DIM

    q = (q_ref[...] * (ATTN_HEAD_DIM ** -0.5)).astype(BF16)
    kcat = jnp.concatenate([kp_ref[...], kc_ref[...]], axis=0)
    vcat = jnp.concatenate([vp_ref[...], vc_ref[...]], axis=0)

    for m in range(N_KV_HEADS // 2):
        kg = kcat[:, m * LANES:(m + 1) * LANES]
        vg = vcat[:, m * LANES:(m + 1) * LANES]
        kg_sw = pltpu.roll(kg, ATTN_HEAD_DIM, axis=1)
        vg_sw = pltpu.roll(vg, ATTN_HEAD_DIM, axis=1)
        for t in range(2):
            h = 2 * m + t
            k_src, k_alt = (kg, kg_sw) if t == 0 else (kg_sw, kg)
            v_src, v_alt = (vg, vg_sw) if t == 0 else (vg_sw, vg)
            k_ab = (jnp.where(lo, k_src, 0.0).astype(BF16), jnp.where(lo, 0.0, k_alt).astype(BF16))
            v_ab = (jnp.where(lo, v_src, 0.0).astype(BF16), jnp.where(lo, 0.0, v_alt).astype(BF16))
            for pp in range(2):
                p = 2 * h + pp
                qp = q[:, p * LANES:(p + 1) * LANES]
                out_pair = jnp.zeros((ATTN_BLOCK, LANES), F32)
                for s in range(2):
                    head = 2 * p + s
                    logits = _dot_nt(qp, k_ab[s]) + bias_ref[table, head]
                    sink = sink_ref[head]
                    mx = jnp.maximum(jnp.max(logits, axis=-1, keepdims=True), sink)
                    pr = jnp.exp(logits - mx)
                    den = jnp.sum(pr, axis=-1, keepdims=True) + jnp.exp(sink - mx)
                    out_pair = out_pair + _dot(pr.astype(BF16), v_ab[s]) * (1.0 / den)
                o_ref[:, p * LANES:(p + 1) * LANES] = out_pair.astype(o_ref.dtype)


def _swa_attention(proj, rel_bias, sinks, batch, seq):
    nb = seq // ATTN_BLOCK
    bucket = jnp.asarray(_t5_bucket_table())
    q_blk = COL_Q // ATTN_Q
    k_blk = COL_K // ATTN_KV
    v_blk = COL_V // ATTN_KV

    def cur(col):
        return lambda b, n: (b * nb + n, col)

    def prev(col):
        return lambda b, n: (b * nb + jnp.maximum(n - 1, 0), col)

    smem = pl.BlockSpec(memory_space=pltpu.SMEM)
    return pl.pallas_call(
        _attn_kernel,
        out_shape=jax.ShapeDtypeStruct((batch * seq, ATTN_Q), BF16),
        grid=(batch, nb),
        in_specs=[
            pl.BlockSpec((ATTN_BLOCK, 2 * ATTN_BLOCK), lambda b, n: (0, 0)),
            smem, smem,
            pl.BlockSpec((ATTN_BLOCK, ATTN_Q), cur(q_blk)),
            pl.BlockSpec((ATTN_BLOCK, ATTN_KV), prev(k_blk)),
            pl.BlockSpec((ATTN_BLOCK, ATTN_KV), cur(k_blk)),
            pl.BlockSpec((ATTN_BLOCK, ATTN_KV), prev(v_blk)),
            pl.BlockSpec((ATTN_BLOCK, ATTN_KV), cur(v_blk)),
        ],
        out_specs=pl.BlockSpec((ATTN_BLOCK, ATTN_Q), lambda b, n: (b * nb + n, 0)),
        scratch_shapes=[pltpu.VMEM((2, N_ATTN_HEADS, ATTN_BLOCK, 2 * ATTN_BLOCK), F32)],
        compiler_params=pltpu.CompilerParams(
            dimension_semantics=("arbitrary", "arbitrary"),
            vmem_limit_bytes=VMEM_LIMIT_BYTES),
        name="swa_attn",
    )(bucket, rel_bias, sinks, proj, proj, proj, proj, proj)


def _gdn_kernel(xprev_ref, x_ref, z_ref, ab_ref, convw_ref, alog_ref, dtb_ref, normw_ref,
                o_ref, cbuf_ref, state_ref):
    c = pl.program_id(1)

    @pl.when(c == 0)
    def _():
        state_ref[...] = jnp.zeros(state_ref.shape, F32)

    prev = xprev_ref[...]
    cbuf_ref[0:SUBLANES, :] = jnp.where(c == 0, 0.0, prev)
    cbuf_ref[SUBLANES:SUBLANES + CHUNK, :] = x_ref[...]
    y = jnp.zeros((CHUNK, DELTA_QKV), F32)
    for j in range(CONV_WIDTH):
        off = SUBLANES - (CONV_WIDTH - 1) + j
        y = y + convw_ref[j:j + 1, :] * cbuf_ref[off:off + CHUNK, :]
    y = y * jax.nn.sigmoid(y)

    ab = ab_ref[...]
    g_all = -jnp.exp(alog_ref[...]) * jax.nn.softplus(ab + dtb_ref[...])
    beta_all = jax.nn.sigmoid(ab)
    row = lax.broadcasted_iota(jnp.int32, (CHUNK, CHUNK), 0)
    colm = lax.broadcasted_iota(jnp.int32, (CHUNK, CHUNK), 1)
    tril = row >= colm
    strict = row > colm
    ltri = tril.astype(F32)
    g_cum = jnp.dot(ltri, g_all, precision=lax.Precision.HIGHEST, preferred_element_type=F32)
    g_cum_t = lax.dot_general(g_all, ltri, (((0,), (1,)), ((), ())),
                              precision=lax.Precision.HIGHEST, preferred_element_type=F32)
    eye = (row == colm).astype(F32)
    normw = normw_ref[...]

    for h in range(N_DELTA_HEADS):
        sl = slice(h * DELTA_HEAD_DIM, (h + 1) * DELTA_HEAD_DIM)
        qh = y[:, sl]
        kh = y[:, DELTA_WIDTH + h * DELTA_HEAD_DIM:DELTA_WIDTH + (h + 1) * DELTA_HEAD_DIM]
        vh = y[:, 2 * DELTA_WIDTH + h * DELTA_HEAD_DIM:2 * DELTA_WIDTH + (h + 1) * DELTA_HEAD_DIM]
        qh = qh * lax.rsqrt(jnp.sum(qh * qh, axis=-1, keepdims=True) + RMS_EPS) * (DELTA_HEAD_DIM ** -0.5)
        kh = kh * lax.rsqrt(jnp.sum(kh * kh, axis=-1, keepdims=True) + RMS_EPS)

        beta = beta_all[:, N_DELTA_HEADS + h:N_DELTA_HEADS + h + 1]
        gc = g_cum[:, h:h + 1]
        gr = g_cum_t[h:h + 1, :]
        g_last = gc[CHUNK - 1:CHUNK, :]
        exp_g = jnp.exp(gc)
        decay = jnp.exp(jnp.where(tril, gc - gr, -jnp.inf))

        kb = kh * beta
        k_bf = kh.astype(BF16)
        kq = _dot_nt(jnp.concatenate([kb, qh], axis=0).astype(BF16), k_bf)
        a_mat = jnp.where(strict, kq[0:CHUNK] * decay, 0.0)
        attn = kq[CHUNK:2 * CHUNK] * decay

        nj = -a_mat
        pj = eye + nj
        nj = _dot(nj.astype(BF16), nj.astype(BF16))
        for _ in range(4):
            both = _dot(jnp.concatenate([pj, nj], axis=0).astype(BF16), nj.astype(BF16))
            pj = pj + both[0:CHUNK]
            nj = both[CHUNK:2 * CHUNK]
        pj = pj + _dot(pj.astype(BF16), nj.astype(BF16))

        rhs = jnp.concatenate([vh * beta, kb * exp_g], axis=1).astype(BF16)
        sol = _dot(pj.astype(BF16), rhs)
        u = sol[:, 0:DELTA_HEAD_DIM]
        w = sol[:, DELTA_HEAD_DIM:2 * DELTA_HEAD_DIM]

        state = state_ref[h]
        s_bf = state.astype(BF16)
        q_dec = qh * exp_g
        k_dec = kh * jnp.exp(g_last - gc)
        ws = _dot(jnp.concatenate([w, q_dec], axis=0).astype(BF16), s_bf)
        v_new = u - ws[0:CHUNK]
        v_bf = v_new.astype(BF16)
        o = ws[CHUNK:2 * CHUNK] + _dot(attn.astype(BF16), v_bf)
        state_ref[h] = state * jnp.exp(g_last) + _dot_tn(k_dec.astype(BF16), v_bf)

        o = o * lax.rsqrt(jnp.mean(o * o, axis=-1, keepdims=True) + RMS_EPS) * normw
        zh = z_ref[:, sl]
        o_ref[:, sl] = (o * (zh * jax.nn.sigmoid(zh))).astype(o_ref.dtype)


def _gdn(proj, conv_w, alog_row, dtb_row, normw_row, batch, seq):
    nc = seq // CHUNK
    rows_per_prev = CHUNK // SUBLANES

    def cur(col):
        return lambda b, c: (b * nc + c, col)

    def prev_rows(b, c):
        return (jnp.maximum((b * nc + c) * rows_per_prev - 1, 0), COL_DQKV // DELTA_QKV)

    const = lambda b, c: (0, 0)
    return pl.pallas_call(
        _gdn_kernel,
        out_shape=jax.ShapeDtypeStruct((batch * seq, DELTA_WIDTH), BF16),
        grid=(batch, nc),
        in_specs=[
            pl.BlockSpec((SUBLANES, DELTA_QKV), prev_rows),
            pl.BlockSpec((CHUNK, DELTA_QKV), cur(COL_DQKV // DELTA_QKV)),
            pl.BlockSpec((CHUNK, DELTA_WIDTH), cur(COL_Z // DELTA_WIDTH)),
            pl.BlockSpec((CHUNK, LANES), cur(COL_AB // LANES)),
            pl.BlockSpec((CONV_WIDTH, DELTA_QKV), const),
            pl.BlockSpec((1, LANES), const),
            pl.BlockSpec((1, LANES), const),
            pl.BlockSpec((1, DELTA_HEAD_DIM), const),
        ],
        out_specs=pl.BlockSpec((CHUNK, DELTA_WIDTH), lambda b, c: (b * nc + c, 0)),
        scratch_shapes=[
            pltpu.VMEM((SUBLANES + CHUNK, DELTA_QKV), F32),
            pltpu.VMEM((N_DELTA_HEADS, DELTA_HEAD_DIM, DELTA_HEAD_DIM), F32),
        ],
        compiler_params=pltpu.CompilerParams(
            dimension_semantics=("arbitrary", "arbitrary"),
            vmem_limit_bytes=VMEM_LIMIT_BYTES),
        name="gdn",
    )(proj, proj, proj, proj, conv_w, alog_row, dtb_row, normw_row)


OUT_TM = 512


def _out_ln1_kernel(x_ref, attn_ref, delta_ref, wo_ref, g_ref, b_ref, o_ref):
    mixed = (_dot(attn_ref[...], wo_ref[0:ATTN_Q, :])
             + _dot(delta_ref[...], wo_ref[ATTN_Q:ATTN_Q + DELTA_WIDTH, :]))
    y = DN_ALPHA * x_ref[...] + mixed
    o_ref[...] = _layer_norm(y, g_ref[...], b_ref[...])


def _out_ln1(x2, attn_out, delta_out, w_o, g, b):
    t = x2.shape[0]
    const = lambda i: (0, 0)
    return pl.pallas_call(
        _out_ln1_kernel,
        out_shape=jax.ShapeDtypeStruct((t, D_MODEL), F32),
        grid=(t // OUT_TM,),
        in_specs=[
            pl.BlockSpec((OUT_TM, D_MODEL), lambda i: (i, 0)),
            pl.BlockSpec((OUT_TM, ATTN_Q), lambda i: (i, 0)),
            pl.BlockSpec((OUT_TM, DELTA_WIDTH), lambda i: (i, 0)),
            pl.BlockSpec((ATTN_Q + DELTA_WIDTH, D_MODEL), const),
            pl.BlockSpec((1, D_MODEL), const),
            pl.BlockSpec((1, D_MODEL), const),
        ],
        out_specs=pl.BlockSpec((OUT_TM, D_MODEL), lambda i: (i, 0)),
        compiler_params=pltpu.CompilerParams(
            dimension_semantics=("arbitrary",),
            vmem_limit_bytes=VMEM_LIMIT_BYTES),
        name="out_ln1",
    )(x2, attn_out, delta_out, w_o, g, b)


MLP_TM = 1024
MLP_TF = 512


def _mlp_ln2_kernel(x_ref, wup_ref, wdn_ref, g_ref, b_ref, o_ref, xb_ref):
    j = pl.program_id(1)

    @pl.when(j == 0)
    def _():
        x = x_ref[...]
        xb_ref[...] = x.astype(BF16)
        o_ref[...] = DN_ALPHA * x

    a = jnp.maximum(_dot(xb_ref[...], wup_ref[...]), 0.0)
    o_ref[...] += _dot((a * a).astype(BF16), wdn_ref[...])

    @pl.when(j == pl.num_programs(1) - 1)
    def _():
        o_ref[...] = _layer_norm(o_ref[...], g_ref[...], b_ref[...])


def _mlp_ln2(x1, w_up, w_down, g, b):
    t = x1.shape[0]
    const = lambda i, j: (0, 0)
    return pl.pallas_call(
        _mlp_ln2_kernel,
        out_shape=jax.ShapeDtypeStruct((t, D_MODEL), F32),
        grid=(t // MLP_TM, D_FF // MLP_TF),
        in_specs=[
            pl.BlockSpec((MLP_TM, D_MODEL), lambda i, j: (i, 0)),
            pl.BlockSpec((D_MODEL, MLP_TF), lambda i, j: (0, j)),
            pl.BlockSpec((MLP_TF, D_MODEL), lambda i, j: (j, 0)),
            pl.BlockSpec((1, D_MODEL), const),
            pl.BlockSpec((1, D_MODEL), const),
        ],
        out_specs=pl.BlockSpec((MLP_TM, D_MODEL), lambda i, j: (i, 0)),
        scratch_shapes=[pltpu.VMEM((MLP_TM, D_MODEL), BF16)],
        compiler_params=pltpu.CompilerParams(
            dimension_semantics=("arbitrary", "arbitrary"),
            vmem_limit_bytes=VMEM_LIMIT_BYTES),
        name="mlp_ln2",
    )(x1, w_up, w_down, g, b)


def _regroup_w_in(w_in):
    s0 = ATTN_Q
    s1 = s0 + ATTN_KV
    s2 = s1 + ATTN_KV
    s3 = s2 + DELTA_QKV
    s4 = s3 + 2 * N_DELTA_HEADS
    w_q, w_k, w_v = w_in[:, :s0], w_in[:, s0:s1], w_in[:, s1:s2]
    w_d, w_ab, w_z = w_in[:, s2:s3], w_in[:, s3:s4], w_in[:, s4:]
    pad = jnp.zeros((w_in.shape[0], LANES - 2 * N_DELTA_HEADS), w_in.dtype)
    return jnp.concatenate([w_d, w_z, w_q, w_k, w_v, w_ab, pad], axis=1).astype(BF16)


def _lane_row(v):
    return jnp.zeros((1, LANES), F32).at[0, :v.shape[0]].set(v.astype(F32))


def kernel(x, w_in, conv_w, a_log, dt_bias, delta_norm_w, attn_sinks, rel_bias, w_o, ln1_g, ln1_b,
           w_up, w_down, ln2_g, ln2_b):
    batch, seq, d = x.shape
    assert d == D_MODEL and seq % ATTN_BLOCK == 0 and (batch * seq) % IN_TM == 0
    assert w_in.shape[0] == DEPTH
    x2 = x.reshape(batch * seq, d)
    for l in range(DEPTH):
        proj = _in_proj(x2, _regroup_w_in(w_in[l]))
        attn_out = _swa_attention(proj, rel_bias.astype(F32), attn_sinks[l].astype(F32), batch, seq)
        delta_out = _gdn(proj, conv_w[l].reshape(CONV_WIDTH, DELTA_QKV).astype(F32),
                         _lane_row(a_log[l]), _lane_row(dt_bias[l]),
                         delta_norm_w[l].reshape(1, DELTA_HEAD_DIM).astype(F32), batch, seq)
        x1 = _out_ln1(x2, attn_out, delta_out, w_o[l].astype(BF16),
                      ln1_g[l].reshape(1, d), ln1_b[l].reshape(1, d))
        x2 = _mlp_ln2(x1, w_up[l].astype(BF16), w_down[l].astype(BF16),
                      ln2_g[l].reshape(1, d), ln2_b[l].reshape(1, d))
    return x2.reshape(batch, seq, d)
```

```python
import functools
import math

import jax
import jax.numpy as jnp
import numpy as np
from jax import lax
from jax.experimental import pallas as pl
from jax.experimental.pallas import tpu as pltpu

F32 = jnp.float32
BF16 = jnp.bfloat16

D_MODEL = 2048
ATTN_HEAD_DIM = 64
N_ATTN_HEADS = 16
N_KV_HEADS = 4
ATTN_BLOCK = 128
WINDOW = 128
NEG_INF = -1e30
N_BUCKETS = 32
MAX_DISTANCE = 128
DELTA_HEAD_DIM = 128
N_DELTA_HEADS = 8
DELTA_WIDTH = N_DELTA_HEADS * DELTA_HEAD_DIM
CONV_WIDTH = 4
CHUNK = 64
D_FF = 4 * D_MODEL
DEPTH = 1
DN_ALPHA = (2.0 * DEPTH) ** 0.25
LN_EPS = 1e-5
RMS_EPS = 1e-6

ATTN_Q = N_ATTN_HEADS * ATTN_HEAD_DIM
ATTN_KV = N_KV_HEADS * ATTN_HEAD_DIM
DELTA_QKV = 3 * DELTA_WIDTH

LANES = 128
SUBLANES = 8
VMEM_LIMIT_BYTES = 56 * 1024 * 1024

COL_DQKV = 0
COL_Z = COL_DQKV + DELTA_QKV
COL_Q = COL_Z + DELTA_WIDTH
COL_K = COL_Q + ATTN_Q
COL_V = COL_K + ATTN_KV
COL_AB = COL_V + ATTN_KV
PROJ_COLS = COL_AB + LANES


def _dot(a, b):
    return jnp.dot(a, b, preferred_element_type=F32)


def _dot_nt(a, b):
    return lax.dot_general(a, b, (((1,), (1,)), ((), ())), preferred_element_type=F32)


def _dot_tn(a, b):
    return lax.dot_general(a, b, (((0,), (0,)), ((), ())), preferred_element_type=F32)


def _layer_norm(y, g, b):
    mu = jnp.mean(y, axis=-1, keepdims=True)
    yc = y - mu
    var = jnp.mean(yc * yc, axis=-1, keepdims=True)
    return yc * lax.rsqrt(var + LN_EPS) * g + b


IN_TM = 1024
IN_TN = 640


def _in_proj_kernel(x_ref, w_ref, o_ref, xb_ref):
    @pl.when(pl.program_id(1) == 0)
    def _():
        xb_ref[...] = x_ref[...].astype(BF16)

    o_ref[...] = _dot(xb_ref[...], w_ref[...])


def _in_proj(x2, w_all):
    t = x2.shape[0]
    return pl.pallas_call(
        _in_proj_kernel,
        out_shape=jax.ShapeDtypeStruct((t, PROJ_COLS), F32),
        grid=(t // IN_TM, PROJ_COLS // IN_TN),
        in_specs=[
            pl.BlockSpec((IN_TM, D_MODEL), lambda i, j: (i, 0)),
            pl.BlockSpec((D_MODEL, IN_TN), lambda i, j: (0, j)),
        ],
        out_specs=pl.BlockSpec((IN_TM, IN_TN), lambda i, j: (i, j)),
        scratch_shapes=[pltpu.VMEM((IN_TM, D_MODEL), BF16)],
        compiler_params=pltpu.CompilerParams(
            dimension_semantics=("arbitrary", "arbitrary"),
            vmem_limit_bytes=VMEM_LIMIT_BYTES),
        name="in_proj",
    )(x2, w_all)


def _t5_bucket_table():
    qi = np.arange(ATTN_BLOCK, dtype=np.int64)[:, None]
    kj = np.arange(2 * ATTN_BLOCK, dtype=np.int64)[None, :]
    dist = qi + ATTN_BLOCK - kj
    n = np.maximum(dist, 0)
    max_exact = N_BUCKETS // 2
    nf = np.maximum(n, 1).astype(np.float64)
    large = max_exact + (np.log(nf / max_exact) / math.log(MAX_DISTANCE / max_exact)
                         * (N_BUCKETS - max_exact)).astype(np.int64)
    large = np.minimum(large, N_BUCKETS - 1)
    bucket = np.where(n < max_exact, n, large)
    band = (dist >= 0) & (dist < WINDOW)
    return np.where(band, bucket, -1).astype(np.int32)


def _attn_kernel(bucket_ref, relb_ref, sink_ref, q_ref, kp_ref, kc_ref, vp_ref, vc_ref,
                 o_ref, bias_ref):
    b = pl.program_id(0)
    n = pl.program_id(1)

    @pl.when((b == 0) & (n == 0))
    def _():
        bucket = bucket_ref[...]
        col = lax.broadcasted_iota(jnp.int32, bucket.shape, 1)

        def head_body(h, carry):
            acc = jnp.full(bucket.shape, NEG_INF, F32)
            for bk in range(N_BUCKETS):
                acc = jnp.where(bucket == bk, relb_ref[bk, h], acc)
            bias_ref[0, h] = acc
            bias_ref[1, h] = jnp.where(col < ATTN_BLOCK, NEG_INF, acc)
            return carry

        lax.fori_loop(0, N_ATTN_HEADS, head_body, 0)

    table = jnp.where(n == 0, 1, 0)
    lane = lax.broadcasted_iota(jnp.int32, (2 * ATTN_BLOCK, LANES), 1)
    lo = lane < ATTN_HEAD_DIM

    q = (q_ref[...] * (ATTN_HEAD_DIM ** -0.5)).astype(BF16)
    kcat = jnp.concatenate([kp_ref[...], kc_ref[...]], axis=0)
    vcat = jnp.concatenate([vp_ref[...], vc_ref[...]], axis=0)

    for m in range(N_KV_HEADS // 2):
        kg = kcat[:, m * LANES:(m + 1) * LANES]
        vg = vcat[:, m * LANES:(m + 1) * LANES]
        kg_sw = pltpu.roll(kg, ATTN_HEAD_DIM, axis=1)
        vg_sw = pltpu.roll(vg, ATTN_HEAD_DIM, axis=1)
        for t in range(2):
            h = 2 * m + t
            k_src, k_alt = (kg, kg_sw) if t == 0 else (kg_sw, kg)
            v_src, v_alt = (vg, vg_sw) if t == 0 else (vg_sw, vg)
            k_ab = (jnp.where(lo, k_src, 0.0).astype(BF16), jnp.where(lo, 0.0, k_alt).astype(BF16))
            v_ab = (jnp.where(lo, v_src, 0.0).astype(BF16), jnp.where(lo, 0.0, v_alt).astype(BF16))
            for pp in range(2):
                p = 2 * h + pp
                qp = q[:, p * LANES:(p + 1) * LANES]
                out_pair = jnp.zeros((ATTN_BLOCK, LANES), F32)
                for s in range(2):
                    head = 2 * p + s
                    logits = _dot_nt(qp, k_ab[s]) + bias_ref[table, head]
                    sink = sink_ref[head]
                    mx = jnp.maximum(jnp.max(logits, axis=-1, keepdims=True), sink)
                    pr = jnp.exp(logits - mx)
                    den = jnp.sum(pr, axis=-1, keepdims=True) + jnp.exp(sink - mx)
                    out_pair = out_pair + _dot(pr.astype(BF16), v_ab[s]) * (1.0 / den)
                o_ref[:, p * LANES:(p + 1) * LANES] = out_pair.astype(o_ref.dtype)


def _swa_attention(proj, rel_bias, sinks, batch, seq):
    nb = seq // ATTN_BLOCK
    bucket = jnp.asarray(_t5_bucket_table())
    q_blk = COL_Q // ATTN_Q
    k_blk = COL_K // ATTN_KV
    v_blk = COL_V // ATTN_KV

    def cur(col):
        return lambda b, n: (b * nb + n, col)

    def prev(col):
        return lambda b, n: (b * nb + jnp.maximum(n - 1, 0), col)

    smem = pl.BlockSpec(memory_space=pltpu.SMEM)
    return pl.pallas_call(
        _attn_kernel,
        out_shape=jax.ShapeDtypeStruct((batch * seq, ATTN_Q), BF16),
        grid=(batch, nb),
        in_specs=[
            pl.BlockSpec((ATTN_BLOCK, 2 * ATTN_BLOCK), lambda b, n: (0, 0)),
            smem, smem,
            pl.BlockSpec((ATTN_BLOCK, ATTN_Q), cur(q_blk)),
            pl.BlockSpec((ATTN_BLOCK, ATTN_KV), prev(k_blk)),
            pl.BlockSpec((ATTN_BLOCK, ATTN_KV), cur(k_blk)),
            pl.BlockSpec((ATTN_BLOCK, ATTN_KV), prev(v_blk)),
            pl.BlockSpec((ATTN_BLOCK, ATTN_KV), cur(v_blk)),
        ],
        out_specs=pl.BlockSpec((ATTN_BLOCK, ATTN_Q), lambda b, n: (b * nb + n, 0)),
        scratch_shapes=[pltpu.VMEM((2, N_ATTN_HEADS, ATTN_BLOCK, 2 * ATTN_BLOCK), F32)],
        compiler_params=pltpu.CompilerParams(
            dimension_semantics=("arbitrary", "arbitrary"),
            vmem_limit_bytes=VMEM_LIMIT_BYTES),
        name="swa_attn",
    )(bucket, rel_bias, sinks, proj, proj, proj, proj, proj)


def _gdn_kernel(xprev_ref, x_ref, z_ref, ab_ref, convw_ref, alog_ref, dtb_ref, normw_ref,
                o_ref, cbuf_ref, state_ref):
    c = pl.program_id(1)

    @pl.when(c == 0)
    def _():
        state_ref[...] = jnp.zeros(state_ref.shape, F32)

    prev = xprev_ref[...]
    cbuf_ref[0:SUBLANES, :] = jnp.where(c == 0, 0.0, prev)
    cbuf_ref[SUBLANES:SUBLANES + CHUNK, :] = x_ref[...]
    y = jnp.zeros((CHUNK, DELTA_QKV), F32)
    for j in range(CONV_WIDTH):
        off = SUBLANES - (CONV_WIDTH - 1) + j
        y = y + convw_ref[j:j + 1, :] * cbuf_ref[off:off + CHUNK, :]
    y = y * jax.nn.sigmoid(y)

    ab = ab_ref[...]
    g_all = -jnp.exp(alog_ref[...]) * jax.nn.softplus(ab + dtb_ref[...])
    beta_all = jax.nn.sigmoid(ab)
    row = lax.broadcasted_iota(jnp.int32, (CHUNK, CHUNK), 0)
    colm = lax.broadcasted_iota(jnp.int32, (CHUNK, CHUNK), 1)
    tril = row >= colm
    strict = row > colm
    ltri = tril.astype(F32)
    g_cum = jnp.dot(ltri, g_all, precision=lax.Precision.HIGHEST, preferred_element_type=F32)
    g_cum_t = lax.dot_general(g_all, ltri, (((0,), (1,)), ((), ())),
                              precision=lax.Precision.HIGHEST, preferred_element_type=F32)
    eye = (row == colm).astype(F32)
    normw = normw_ref[...]

    heads = range(N_DELTA_HEADS)
    d = DELTA_HEAD_DIM

    def l2n(t):
        return t * lax.rsqrt(jnp.sum(t * t, axis=-1, keepdims=True) + RMS_EPS)

    q = [l2n(y[:, h * d:(h + 1) * d]) * (d ** -0.5) for h in heads]
    k = [l2n(y[:, DELTA_WIDTH + h * d:DELTA_WIDTH + (h + 1) * d]) for h in heads]
    v = [y[:, 2 * DELTA_WIDTH + h * d:2 * DELTA_WIDTH + (h + 1) * d] for h in heads]
    beta = [beta_all[:, N_DELTA_HEADS + h:N_DELTA_HEADS + h + 1] for h in heads]
    gc = [g_cum[:, h:h + 1] for h in heads]
    g_last = [gc[h][CHUNK - 1:CHUNK, :] for h in heads]
    exp_g = [jnp.exp(gc[h]) for h in heads]
    decay = [jnp.exp(jnp.where(tril, gc[h] - g_cum_t[h:h + 1, :], -jnp.inf)) for h in heads]

    kb = [k[h] * beta[h] for h in heads]
    k_bf = [k[h].astype(BF16) for h in heads]
    kq = [_dot_nt(jnp.concatenate([kb[h], q[h]], axis=0).astype(BF16), k_bf[h]) for h in heads]
    attn = [(kq[h][CHUNK:2 * CHUNK] * decay[h]).astype(BF16) for h in heads]

    nj = [jnp.where(strict, -(kq[h][0:CHUNK] * decay[h]), 0.0) for h in heads]
    pj = [eye + nj[h] for h in heads]
    nj = [_dot(nj[h].astype(BF16), nj[h].astype(BF16)) for h in heads]
    for _ in range(4):
        both = [_dot(jnp.concatenate([pj[h], nj[h]], axis=0).astype(BF16), nj[h].astype(BF16))
                for h in heads]
        pj = [pj[h] + both[h][0:CHUNK] for h in heads]
        nj = [both[h][CHUNK:2 * CHUNK] for h in heads]
    pj = [pj[h] + _dot(pj[h].astype(BF16), nj[h].astype(BF16)) for h in heads]

    sol = [_dot(pj[h].astype(BF16),
                jnp.concatenate([v[h] * beta[h], kb[h] * exp_g[h]], axis=1).astype(BF16))
           for h in heads]
    wq = [jnp.concatenate([sol[h][:, d:2 * d], q[h] * exp_g[h]], axis=0).astype(BF16) for h in heads]
    k_dec = [(k[h] * jnp.exp(g_last[h] - gc[h])).astype(BF16) for h in heads]

    state = [state_ref[h] for h in heads]
    ws = [_dot(wq[h], state[h].astype(BF16)) for h in heads]
    v_new = [(sol[h][:, 0:d] - ws[h][0:CHUNK]).astype(BF16) for h in heads]
    o = [ws[h][CHUNK:2 * CHUNK] + _dot(attn[h], v_new[h]) for h in heads]
    for h in heads:
        state_ref[h] = state[h] * jnp.exp(g_last[h]) + _dot_tn(k_dec[h], v_new[h])

    for h in heads:
        oh = o[h] * lax.rsqrt(jnp.mean(o[h] * o[h], axis=-1, keepdims=True) + RMS_EPS) * normw
        zh = z_ref[:, h * d:(h + 1) * d]
        o_ref[:, h * d:(h + 1) * d] = (oh * (zh * jax.nn.sigmoid(zh))).astype(o_ref.dtype)


def _gdn(proj, conv_w, alog_row, dtb_row, normw_row, batch, seq):
    nc = seq // CHUNK
    rows_per_prev = CHUNK // SUBLANES

    def cur(col):
        return lambda b, c: (b * nc + c, col)

    def prev_rows(b, c):
        return (jnp.maximum((b * nc + c) * rows_per_prev - 1, 0), COL_DQKV // DELTA_QKV)

    const = lambda b, c: (0, 0)
    return pl.pallas_call(
        _gdn_kernel,
        out_shape=jax.ShapeDtypeStruct((batch * seq, DELTA_WIDTH), BF16),
        grid=(batch, nc),
        in_specs=[
            pl.BlockSpec((SUBLANES, DELTA_QKV), prev_rows),
            pl.BlockSpec((CHUNK, DELTA_QKV), cur(COL_DQKV // DELTA_QKV)),
            pl.BlockSpec((CHUNK, DELTA_WIDTH), cur(COL_Z // DELTA_WIDTH)),
            pl.BlockSpec((CHUNK, LANES), cur(COL_AB // LANES)),
            pl.BlockSpec((CONV_WIDTH, DELTA_QKV), const),
            pl.BlockSpec((1, LANES), const),
            pl.BlockSpec((1, LANES), const),
            pl.BlockSpec((1, DELTA_HEAD_DIM), const),
        ],
        out_specs=pl.BlockSpec((CHUNK, DELTA_WIDTH), lambda b, c: (b * nc + c, 0)),
        scratch_shapes=[
            pltpu.VMEM((SUBLANES + CHUNK, DELTA_QKV), F32),
            pltpu.VMEM((N_DELTA_HEADS, DELTA_HEAD_DIM, DELTA_HEAD_DIM), F32),
        ],
        compiler_params=pltpu.CompilerParams(
            dimension_semantics=("arbitrary", "arbitrary"),
            vmem_limit_bytes=VMEM_LIMIT_BYTES),
        name="gdn",
    )(proj, proj, proj, proj, conv_w, alog_row, dtb_row, normw_row)


OUT_TM = 512


def _out_ln1_kernel(x_ref, attn_ref, delta_ref, wo_ref, g_ref, b_ref, o_ref):
    mixed = (_dot(attn_ref[...], wo_ref[0:ATTN_Q, :])
             + _dot(delta_ref[...], wo_ref[ATTN_Q:ATTN_Q + DELTA_WIDTH, :]))
    y = DN_ALPHA * x_ref[...] + mixed
    o_ref[...] = _layer_norm(y, g_ref[...], b_ref[...])


def _out_ln1(x2, attn_out, delta_out, w_o, g, b):
    t = x2.shape[0]
    const = lambda i: (0, 0)
    return pl.pallas_call(
        _out_ln1_kernel,
        out_shape=jax.ShapeDtypeStruct((t, D_MODEL), F32),
        grid=(t // OUT_TM,),
        in_specs=[
            pl.BlockSpec((OUT_TM, D_MODEL), lambda i: (i, 0)),
            pl.BlockSpec((OUT_TM, ATTN_Q), lambda i: (i, 0)),
            pl.BlockSpec((OUT_TM, DELTA_WIDTH), lambda i: (i, 0)),
            pl.BlockSpec((ATTN_Q + DELTA_WIDTH, D_MODEL), const),
            pl.BlockSpec((1, D_MODEL), const),
            pl.BlockSpec((1, D_MODEL), const),
        ],
        out_specs=pl.BlockSpec((OUT_TM, D_MODEL), lambda i: (i, 0)),
        compiler_params=pltpu.CompilerParams(
            dimension_semantics=("arbitrary",),
            vmem_limit_bytes=VMEM_LIMIT_BYTES),
        name="out_ln1",
    )(x2, attn_out, delta_out, w_o, g, b)


MLP_TM = 1024
MLP_TF = 512


def _mlp_ln2_kernel(x_ref, wup_ref, wdn_ref, g_ref, b_ref, o_ref, xb_ref):
    j = pl.program_id(1)

    @pl.when(j == 0)
    def _():
        x = x_ref[...]
        xb_ref[...] = x.astype(BF16)
        o_ref[...] = DN_ALPHA * x

    a = jnp.maximum(_dot(xb_ref[...], wup_ref[...]), 0.0)
    o_ref[...] += _dot((a * a).astype(BF16), wdn_ref[...])

    @pl.when(j == pl.num_programs(1) - 1)
    def _():
        o_ref[...] = _layer_norm(o_ref[...], g_ref[...], b_ref[...])


def _mlp_ln2(x1, w_up, w_down, g, b):
    t = x1.shape[0]
    const = lambda i, j: (0, 0)
    return pl.pallas_call(
        _mlp_ln2_kernel,
        out_shape=jax.ShapeDtypeStruct((t, D_MODEL), F32),
        grid=(t // MLP_TM, D_FF // MLP_TF),
        in_specs=[
            pl.BlockSpec((MLP_TM, D_MODEL), lambda i, j: (i, 0)),
            pl.BlockSpec((D_MODEL, MLP_TF), lambda i, j: (0, j)),
            pl.BlockSpec((MLP_TF, D_MODEL), lambda i, j: (j, 0)),
            pl.BlockSpec((1, D_MODEL), const),
            pl.BlockSpec((1, D_MODEL), const),
        ],
        out_specs=pl.BlockSpec((MLP_TM, D_MODEL), lambda i, j: (i, 0)),
        scratch_shapes=[pltpu.VMEM((MLP_TM, D_MODEL), BF16)],
        compiler_params=pltpu.CompilerParams(
            dimension_semantics=("arbitrary", "arbitrary"),
            vmem_limit_bytes=VMEM_LIMIT_BYTES),
        name="mlp_ln2",
    )(x1, w_up, w_down, g, b)


def _regroup_w_in(w_in):
    s0 = ATTN_Q
    s1 = s0 + ATTN_KV
    s2 = s1 + ATTN_KV
    s3 = s2 + DELTA_QKV
    s4 = s3 + 2 * N_DELTA_HEADS
    w_q, w_k, w_v = w_in[:, :s0], w_in[:, s0:s1], w_in[:, s1:s2]
    w_d, w_ab, w_z = w_in[:, s2:s3], w_in[:, s3:s4], w_in[:, s4:]
    pad = jnp.zeros((w_in.shape[0], LANES - 2 * N_DELTA_HEADS), w_in.dtype)
    return jnp.concatenate([w_d, w_z, w_q, w_k, w_v, w_ab, pad], axis=1).astype(BF16)


def _lane_row(v):
    return jnp.zeros((1, LANES), F32).at[0, :v.shape[0]].set(v.astype(F32))


def kernel(x, w_in, conv_w, a_log, dt_bias, delta_norm_w, attn_sinks, rel_bias, w_o, ln1_g, ln1_b,
           w_up, w_down, ln2_g, ln2_b):
    batch, seq, d = x.shape
    assert d == D_MODEL and seq % ATTN_BLOCK == 0 and (batch * seq) % IN_TM == 0
    assert w_in.shape[0] == DEPTH
    x2 = x.reshape(batch * seq, d)
    for l in range(DEPTH):
        proj = _in_proj(x2, _regroup_w_in(w_in[l]))
        attn_out = _swa_attention(proj, rel_bias.astype(F32), attn_sinks[l].astype(F32), batch, seq)
        delta_out = _gdn(proj, conv_w[l].reshape(CONV_WIDTH, DELTA_QKV).astype(F32),
                         _lane_row(a_log[l]), _lane_row(dt_bias[l]),
                         delta_norm_w[l].reshape(1, DELTA_HEAD_DIM).astype(F32), batch, seq)
        x1 = _out_ln1(x2, attn_out, delta_out, w_o[l].astype(BF16),
                      ln1_g[l].reshape(1, d), ln1_b[l].reshape(1, d))
        x2 = _mlp_ln2(x1, w_up[l].astype(BF16), w_down[l].astype(BF16),
                      ln2_g[l].reshape(1, d), ln2_b[l].reshape(1, d))
    return x2.reshape(batch, seq, d)
```

```python
import functools
import math

import jax
import jax.numpy as jnp
import numpy as np
from jax import lax
from jax.experimental import pallas as pl
from jax.experimental.pallas import tpu as pltpu

F32 = jnp.float32
BF16 = jnp.bfloat16

D_MODEL = 2048
ATTN_HEAD_DIM = 64
N_ATTN_HEADS = 16
N_KV_HEADS = 4
ATTN_BLOCK = 128
WINDOW = 128
NEG_INF = -1e30
N_BUCKETS = 32
MAX_DISTANCE = 128
DELTA_HEAD_DIM = 128
N_DELTA_HEADS = 8
DELTA_WIDTH = N_DELTA_HEADS * DELTA_HEAD_DIM
CONV_WIDTH = 4
CHUNK = 64
D_FF = 4 * D_MODEL
DEPTH = 1
DN_ALPHA = (2.0 * DEPTH) ** 0.25
LN_EPS = 1e-5
RMS_EPS = 1e-6

ATTN_Q = N_ATTN_HEADS * ATTN_HEAD_DIM
ATTN_KV = N_KV_HEADS * ATTN_HEAD_DIM
DELTA_QKV = 3 * DELTA_WIDTH

LANES = 128
SUBLANES = 8
PREV_ROWS = 2 * SUBLANES
VMEM_LIMIT_BYTES = 56 * 1024 * 1024

COL_DQKV = 0
COL_Z = COL_DQKV + DELTA_QKV
COL_Q = COL_Z + DELTA_WIDTH
COL_K = COL_Q + ATTN_Q
COL_V = COL_K + ATTN_KV
PROJ_COLS = COL_V + ATTN_KV

SRC_Q = 0
SRC_DQKV = ATTN_Q + 2 * ATTN_KV
SRC_AB = SRC_DQKV + DELTA_QKV
SRC_Z = SRC_AB + 2 * N_DELTA_HEADS
MXU_COLS = 256


def _dot(a, b):
    return jnp.dot(a, b, preferred_element_type=F32)


def _dot_nt(a, b):
    return lax.dot_general(a, b, (((1,), (1,)), ((), ())), preferred_element_type=F32)


def _dot_tn(a, b):
    return lax.dot_general(a, b, (((0,), (0,)), ((), ())), preferred_element_type=F32)


def _layer_norm(y, g, b):
    mu = jnp.mean(y, axis=-1, keepdims=True)
    yc = y - mu
    var = jnp.mean(yc * yc, axis=-1, keepdims=True)
    return yc * lax.rsqrt(var + LN_EPS) * g + b


IN_TM = 512
IN_CHUNK = 512


def _in_proj_kernel(x_ref, wmain_ref, wz_ref, wab_ref, o_ref, ab_ref):
    xb = x_ref[...].astype(BF16)
    pieces = ((wmain_ref, SRC_DQKV, DELTA_QKV, COL_DQKV),
              (wz_ref, 0, DELTA_WIDTH, COL_Z),
              (wmain_ref, SRC_Q, ATTN_Q + 2 * ATTN_KV, COL_Q))
    for w_ref, src, width, dst in pieces:
        for c in range(0, width, IN_CHUNK):
            o_ref[:, dst + c:dst + c + IN_CHUNK] = _dot(
                xb, w_ref[:, src + c:src + c + IN_CHUNK]).astype(o_ref.dtype)
    ab_ref[...] = _dot(xb, wab_ref[...])[:, 0:LANES]


def _in_proj(x2, w_main, w_z, w_ab):
    t = x2.shape[0]
    const = lambda i: (0, 0)
    resident = pl.Buffered(1)
    return pl.pallas_call(
        _in_proj_kernel,
        out_shape=(jax.ShapeDtypeStruct((t, PROJ_COLS), BF16),
                   jax.ShapeDtypeStruct((t, LANES), F32)),
        grid=(t // IN_TM,),
        in_specs=[
            pl.BlockSpec((IN_TM, D_MODEL), lambda i: (i, 0)),
            pl.BlockSpec(w_main.shape, const, pipeline_mode=resident),
            pl.BlockSpec(w_z.shape, const, pipeline_mode=resident),
            pl.BlockSpec(w_ab.shape, const, pipeline_mode=resident),
        ],
        out_specs=(pl.BlockSpec((IN_TM, PROJ_COLS), lambda i: (i, 0)),
                   pl.BlockSpec((IN_TM, LANES), lambda i: (i, 0))),
        compiler_params=pltpu.CompilerParams(
            dimension_semantics=("arbitrary",),
            vmem_limit_bytes=VMEM_LIMIT_BYTES),
        name="in_proj",
    )(x2, w_main, w_z, w_ab)


def _t5_bucket_table():
    qi = np.arange(ATTN_BLOCK, dtype=np.int64)[:, None]
    kj = np.arange(2 * ATTN_BLOCK, dtype=np.int64)[None, :]
    dist = qi + ATTN_BLOCK - kj
    n = np.maximum(dist, 0)
    max_exact = N_BUCKETS // 2
    nf = np.maximum(n, 1).astype(np.float64)
    large = max_exact + (np.log(nf / max_exact) / math.log(MAX_DISTANCE / max_exact)
                         * (N_BUCKETS - max_exact)).astype(np.int64)
    large = np.minimum(large, N_BUCKETS - 1)
    bucket = np.where(n < max_exact, n, large)
    band = (dist >= 0) & (dist < WINDOW)
    return np.where(band, bucket, -1).astype(np.int32)


def _attn_kernel(bucket_ref, relb_ref, sink_ref, q_ref, kp_ref, kc_ref, vp_ref, vc_ref,
                 o_ref, bias_ref):
    b = pl.program_id(0)
    n = pl.program_id(1)

    @pl.when((b == 0) & (n == 0))
    def _():
        bucket = bucket_ref[...]
        col = lax.broadcasted_iota(jnp.int32, bucket.shape, 1)

        def head_body(h, carry):
            acc = jnp.full(bucket.shape, NEG_INF, F32)
            for bk in range(N_BUCKETS):
                acc = jnp.where(bucket == bk, relb_ref[bk, h], acc)
            bias_ref[0, h] = acc
            bias_ref[1, h] = jnp.where(col < ATTN_BLOCK, NEG_INF, acc)
            return carry

        lax.fori_loop(0, N_ATTN_HEADS, head_body, 0)

    table = jnp.where(n == 0, 1, 0)
    lane = lax.broadcasted_iota(jnp.int32, (2 * ATTN_BLOCK, LANES), 1)
    lo = lane < ATTN_HEAD_DIM

    q = (q_ref[...] * (ATTN_HEAD_DIM ** -0.5)).astype(BF16)
    kcat = jnp.concatenate([kp_ref[...], kc_ref[...]], axis=0).astype(F32)
    vcat = jnp.concatenate([vp_ref[...], vc_ref[...]], axis=0).astype(F32)

    for m in range(N_KV_HEADS // 2):
        kg = kcat[:, m * LANES:(m + 1) * LANES]
        vg = vcat[:, m * LANES:(m + 1) * LANES]
        kg_sw = pltpu.roll(kg, ATTN_HEAD_DIM, axis=1)
        vg_sw = pltpu.roll(vg, ATTN_HEAD_DIM, axis=1)
        for t in range(2):
            h = 2 * m + t
            k_src, k_alt = (kg, kg_sw) if t == 0 else (kg_sw, kg)
            v_src, v_alt = (vg, vg_sw) if t == 0 else (vg_sw, vg)
            k_ab = (jnp.where(lo, k_src, 0.0).astype(BF16), jnp.where(lo, 0.0, k_alt).astype(BF16))
            v_ab = (jnp.where(lo, v_src, 0.0).astype(BF16), jnp.where(lo, 0.0, v_alt).astype(BF16))
            for pp in range(2):
                p = 2 * h + pp
                qp = q[:, p * LANES:(p + 1) * LANES]
                out_pair = jnp.zeros((ATTN_BLOCK, LANES), F32)
                for s in range(2):
                    head = 2 * p + s
                    logits = _dot_nt(qp, k_ab[s]) + bias_ref[table, head]
                    sink = sink_ref[head]
                    mx = jnp.maximum(jnp.max(logits, axis=-1, keepdims=True), sink)
                    pr = jnp.exp(logits - mx)
                    den = jnp.sum(pr, axis=-1, keepdims=True) + jnp.exp(sink - mx)
                    out_pair = out_pair + _dot(pr.astype(BF16), v_ab[s]) * (1.0 / den)
                o_ref[:, p * LANES:(p + 1) * LANES] = out_pair.astype(o_ref.dtype)


def _swa_attention(proj, rel_bias, sinks, batch, seq):
    nb = seq // ATTN_BLOCK
    bucket = jnp.asarray(_t5_bucket_table())
    q_blk = COL_Q // ATTN_Q
    k_blk = COL_K // ATTN_KV
    v_blk = COL_V // ATTN_KV

    def cur(col):
        return lambda b, n: (b * nb + n, col)

    def prev(col):
        return lambda b, n: (b * nb + jnp.maximum(n - 1, 0), col)

    smem = pl.BlockSpec(memory_space=pltpu.SMEM)
    return pl.pallas_call(
        _attn_kernel,
        out_shape=jax.ShapeDtypeStruct((batch * seq, ATTN_Q), BF16),
        grid=(batch, nb),
        in_specs=[
            pl.BlockSpec((ATTN_BLOCK, 2 * ATTN_BLOCK), lambda b, n: (0, 0)),
            smem, smem,
            pl.BlockSpec((ATTN_BLOCK, ATTN_Q), cur(q_blk)),
            pl.BlockSpec((ATTN_BLOCK, ATTN_KV), prev(k_blk)),
            pl.BlockSpec((ATTN_BLOCK, ATTN_KV), cur(k_blk)),
            pl.BlockSpec((ATTN_BLOCK, ATTN_KV), prev(v_blk)),
            pl.BlockSpec((ATTN_BLOCK, ATTN_KV), cur(v_blk)),
        ],
        out_specs=pl.BlockSpec((ATTN_BLOCK, ATTN_Q), lambda b, n: (b * nb + n, 0)),
        scratch_shapes=[pltpu.VMEM((2, N_ATTN_HEADS, ATTN_BLOCK, 2 * ATTN_BLOCK), F32)],
        compiler_params=pltpu.CompilerParams(
            dimension_semantics=("arbitrary", "arbitrary"),
            vmem_limit_bytes=VMEM_LIMIT_BYTES),
        name="swa_attn",
    )(bucket, rel_bias, sinks, proj, proj, proj, proj, proj)


def _gdn_kernel(xprev_ref, x_ref, z_ref, ab_ref, convw_ref, alog_ref, dtb_ref, normw_ref,
                o_ref, cbuf_ref, state_ref):
    c = pl.program_id(1)

    @pl.when(c == 0)
    def _():
        state_ref[...] = jnp.zeros(state_ref.shape, F32)

    prev = xprev_ref[...].astype(F32)
    cbuf_ref[0:PREV_ROWS, :] = jnp.where(c == 0, 0.0, prev)
    cbuf_ref[PREV_ROWS:PREV_ROWS + CHUNK, :] = x_ref[...].astype(F32)
    y = jnp.zeros((CHUNK, DELTA_QKV), F32)
    for j in range(CONV_WIDTH):
        off = PREV_ROWS - (CONV_WIDTH - 1) + j
        y = y + convw_ref[j:j + 1, :] * cbuf_ref[off:off + CHUNK, :]
    y = y * jax.nn.sigmoid(y)

    ab = ab_ref[...]
    g_all = -jnp.exp(alog_ref[...]) * jax.nn.softplus(ab + dtb_ref[...])
    beta_all = jax.nn.sigmoid(ab)
    row = lax.broadcasted_iota(jnp.int32, (CHUNK, CHUNK), 0)
    colm = lax.broadcasted_iota(jnp.int32, (CHUNK, CHUNK), 1)
    tril = row >= colm
    strict = row > colm
    ltri = tril.astype(F32)
    g_cum = jnp.dot(ltri, g_all, precision=lax.Precision.HIGHEST, preferred_element_type=F32)
    g_cum_t = lax.dot_general(g_all, ltri, (((0,), (1,)), ((), ())),
                              precision=lax.Precision.HIGHEST, preferred_element_type=F32)
    eye = (row == colm).astype(F32)
    normw = normw_ref[...]

    heads = range(N_DELTA_HEADS)
    d = DELTA_HEAD_DIM

    def l2n(t):
        return t * lax.rsqrt(jnp.sum(t * t, axis=-1, keepdims=True) + RMS_EPS)

    q = [l2n(y[:, h * d:(h + 1) * d]) * (d ** -0.5) for h in heads]
    k = [l2n(y[:, DELTA_WIDTH + h * d:DELTA_WIDTH + (h + 1) * d]) for h in heads]
    v = [y[:, 2 * DELTA_WIDTH + h * d:2 * DELTA_WIDTH + (h + 1) * d] for h in heads]
    beta = [beta_all[:, N_DELTA_HEADS + h:N_DELTA_HEADS + h + 1] for h in heads]
    gc = [g_cum[:, h:h + 1] for h in heads]
    g_last = [gc[h][CHUNK - 1:CHUNK, :] for h in heads]
    exp_g = [jnp.exp(gc[h]) for h in heads]
    decay = [jnp.exp(jnp.where(tril, gc[h] - g_cum_t[h:h + 1, :], -jnp.inf)) for h in heads]

    kb = [k[h] * beta[h] for h in heads]
    k_bf = [k[h].astype(BF16) for h in heads]
    kq = [_dot_nt(jnp.concatenate([kb[h], q[h]], axis=0).astype(BF16), k_bf[h]) for h in heads]
    attn = [(kq[h][CHUNK:2 * CHUNK] * decay[h]).astype(BF16) for h in heads]

    nj = [jnp.where(strict, -(kq[h][0:CHUNK] * decay[h]), 0.0) for h in heads]
    pj = [eye + nj[h] for h in heads]
    nj = [_dot(nj[h].astype(BF16), nj[h].astype(BF16)) for h in heads]
    for _ in range(4):
        both = [_dot(jnp.concatenate([pj[h], nj[h]], axis=0).astype(BF16), nj[h].astype(BF16))
                for h in heads]
        pj = [pj[h] + both[h][0:CHUNK] for h in heads]
        nj = [both[h][CHUNK:2 * CHUNK] for h in heads]
    pj = [pj[h] + _dot(pj[h].astype(BF16), nj[h].astype(BF16)) for h in heads]

    sol = [_dot(pj[h].astype(BF16),
                jnp.concatenate([v[h] * beta[h], kb[h] * exp_g[h]], axis=1).astype(BF16))
           for h in heads]
    wq = [jnp.concatenate([sol[h][:, d:2 * d], q[h] * exp_g[h]], axis=0).astype(BF16) for h in heads]
    k_dec = [(k[h] * jnp.exp(g_last[h] - gc[h])).astype(BF16) for h in heads]

    state = [state_ref[h] for h in heads]
    ws = [_dot(wq[h], state[h].astype(BF16)) for h in heads]
    v_new = [(sol[h][:, 0:d] - ws[h][0:CHUNK]).astype(BF16) for h in heads]
    o = [ws[h][CHUNK:2 * CHUNK] + _dot(attn[h], v_new[h]) for h in heads]
    for h in heads:
        state_ref[h] = state[h] * jnp.exp(g_last[h]) + _dot_tn(k_dec[h], v_new[h])

    for h in heads:
        oh = o[h] * lax.rsqrt(jnp.mean(o[h] * o[h], axis=-1, keepdims=True) + RMS_EPS) * normw
        zh = z_ref[:, h * d:(h + 1) * d].astype(F32)
        o_ref[:, h * d:(h + 1) * d] = (oh * (zh * jax.nn.sigmoid(zh))).astype(o_ref.dtype)


def _gdn(proj, ab, conv_w, alog_row, dtb_row, normw_row, batch, seq):
    nc = seq // CHUNK
    prev_per_chunk = CHUNK // PREV_ROWS

    def cur(col):
        return lambda b, c: (b * nc + c, col)

    def prev_rows(b, c):
        return (jnp.maximum((b * nc + c) * prev_per_chunk - 1, 0), COL_DQKV // DELTA_QKV)

    const = lambda b, c: (0, 0)
    return pl.pallas_call(
        _gdn_kernel,
        out_shape=jax.ShapeDtypeStruct((batch * seq, DELTA_WIDTH), BF16),
        grid=(batch, nc),
        in_specs=[
            pl.BlockSpec((PREV_ROWS, DELTA_QKV), prev_rows),
            pl.BlockSpec((CHUNK, DELTA_QKV), cur(COL_DQKV // DELTA_QKV)),
            pl.BlockSpec((CHUNK, DELTA_WIDTH), cur(COL_Z // DELTA_WIDTH)),
            pl.BlockSpec((CHUNK, LANES), cur(0)),
            pl.BlockSpec((CONV_WIDTH, DELTA_QKV), const),
            pl.BlockSpec((1, LANES), const),
            pl.BlockSpec((1, LANES), const),
            pl.BlockSpec((1, DELTA_HEAD_DIM), const),
        ],
        out_specs=pl.BlockSpec((CHUNK, DELTA_WIDTH), lambda b, c: (b * nc + c, 0)),
        scratch_shapes=[
            pltpu.VMEM((PREV_ROWS + CHUNK, DELTA_QKV), F32),
            pltpu.VMEM((N_DELTA_HEADS, DELTA_HEAD_DIM, DELTA_HEAD_DIM), F32),
        ],
        compiler_params=pltpu.CompilerParams(
            dimension_semantics=("arbitrary", "arbitrary"),
            vmem_limit_bytes=VMEM_LIMIT_BYTES),
        name="gdn",
    )(proj, proj, proj, ab, conv_w, alog_row, dtb_row, normw_row)


OUT_TM = 512


def _out_ln1_kernel(x_ref, attn_ref, delta_ref, wo_ref, g_ref, b_ref, o_ref):
    mixed = (_dot(attn_ref[...], wo_ref[0:ATTN_Q, :])
             + _dot(delta_ref[...], wo_ref[ATTN_Q:ATTN_Q + DELTA_WIDTH, :]))
    y = DN_ALPHA * x_ref[...] + mixed
    o_ref[...] = _layer_norm(y, g_ref[...], b_ref[...])


def _out_ln1(x2, attn_out, delta_out, w_o, g, b):
    t = x2.shape[0]
    const = lambda i: (0, 0)
    return pl.pallas_call(
        _out_ln1_kernel,
        out_shape=jax.ShapeDtypeStruct((t, D_MODEL), F32),
        grid=(t // OUT_TM,),
        in_specs=[
            pl.BlockSpec((OUT_TM, D_MODEL), lambda i: (i, 0)),
            pl.BlockSpec((OUT_TM, ATTN_Q), lambda i: (i, 0)),
            pl.BlockSpec((OUT_TM, DELTA_WIDTH), lambda i: (i, 0)),
            pl.BlockSpec((ATTN_Q + DELTA_WIDTH, D_MODEL), const),
            pl.BlockSpec((1, D_MODEL), const),
            pl.BlockSpec((1, D_MODEL), const),
        ],
        out_specs=pl.BlockSpec((OUT_TM, D_MODEL), lambda i: (i, 0)),
        compiler_params=pltpu.CompilerParams(
            dimension_semantics=("arbitrary",),
            vmem_limit_bytes=VMEM_LIMIT_BYTES),
        name="out_ln1",
    )(x2, attn_out, delta_out, w_o, g, b)


MLP_TM = 1024
MLP_TF = 512


def _mlp_ln2_kernel(x_ref, wup_ref, wdn_ref, g_ref, b_ref, o_ref, xb_ref):
    j = pl.program_id(1)

    @pl.when(j == 0)
    def _():
        x = x_ref[...]
        xb_ref[...] = x.astype(BF16)
        o_ref[...] = DN_ALPHA * x

    a = jnp.maximum(_dot(xb_ref[...], wup_ref[...]), 0.0)
    o_ref[...] += _dot((a * a).astype(BF16), wdn_ref[...])

    @pl.when(j == pl.num_programs(1) - 1)
    def _():
        o_ref[...] = _layer_norm(o_ref[...], g_ref[...], b_ref[...])


def _mlp_ln2(x1, w_up, w_down, g, b):
    t = x1.shape[0]
    const = lambda i, j: (0, 0)
    return pl.pallas_call(
        _mlp_ln2_kernel,
        out_shape=jax.ShapeDtypeStruct((t, D_MODEL), F32),
        grid=(t // MLP_TM, D_FF // MLP_TF),
        in_specs=[
            pl.BlockSpec((MLP_TM, D_MODEL), lambda i, j: (i, 0)),
            pl.BlockSpec((D_MODEL, MLP_TF), lambda i, j: (0, j)),
            pl.BlockSpec((MLP_TF, D_MODEL), lambda i, j: (j, 0)),
            pl.BlockSpec((1, D_MODEL), const),
            pl.BlockSpec((1, D_MODEL), const),
        ],
        out_specs=pl.BlockSpec((MLP_TM, D_MODEL), lambda i, j: (i, 0)),
        scratch_shapes=[pltpu.VMEM((MLP_TM, D_MODEL), BF16)],
        compiler_params=pltpu.CompilerParams(
            dimension_semantics=("arbitrary", "arbitrary"),
            vmem_limit_bytes=VMEM_LIMIT_BYTES),
        name="mlp_ln2",
    )(x1, w_up, w_down, g, b)


def _split_w_in(w_in):
    w_main = w_in[:, :SRC_AB].astype(BF16)
    w_z = w_in[:, SRC_Z:].astype(BF16)
    w_ab = jnp.pad(w_in[:, SRC_AB:SRC_Z], ((0, 0), (0, MXU_COLS - 2 * N_DELTA_HEADS))).astype(BF16)
    return w_main, w_z, w_ab


def _lane_row(v):
    return jnp.zeros((1, LANES), F32).at[0, :v.shape[0]].set(v.astype(F32))


def kernel(x, w_in, conv_w, a_log, dt_bias, delta_norm_w, attn_sinks, rel_bias, w_o, ln1_g, ln1_b,
           w_up, w_down, ln2_g, ln2_b):
    batch, seq, d = x.shape
    assert d == D_MODEL and seq % ATTN_BLOCK == 0 and (batch * seq) % IN_TM == 0
    assert w_in.shape[0] == DEPTH
    x2 = x.reshape(batch * seq, d)
    for l in range(DEPTH):
        proj, ab = _in_proj(x2, *_split_w_in(w_in[l]))
        attn_out = _swa_attention(proj, rel_bias.astype(F32), attn_sinks[l].astype(F32), batch, seq)
        delta_out = _gdn(proj, ab, conv_w[l].reshape(CONV_WIDTH, DELTA_QKV).astype(F32),
                         _lane_row(a_log[l]), _lane_row(dt_bias[l]),
                         delta_norm_w[l].reshape(1, DELTA_HEAD_DIM).astype(F32), batch, seq)
        x1 = _out_ln1(x2, attn_out, delta_out, w_o[l].astype(BF16),
                      ln1_g[l].reshape(1, d), ln1_b[l].reshape(1, d))
        x2 = _mlp_ln2(x1, w_up[l].astype(BF16), w_down[l].astype(BF16),
                      ln2_g[l].reshape(1, d), ln2_b[l].reshape(1, d))
    return x2.reshape(batch, seq, d)
```

```python
import functools
import math

import jax
import jax.numpy as jnp
import numpy as np
from jax import lax
from jax.experimental import pallas as pl
from jax.experimental.pallas import tpu as pltpu

F32 = jnp.float32
BF16 = jnp.bfloat16

D_MODEL = 2048
ATTN_HEAD_DIM = 64
N_ATTN_HEADS = 16
N_KV_HEADS = 4
ATTN_BLOCK = 128
WINDOW = 128
NEG_INF = -1e30
N_BUCKETS = 32
MAX_DISTANCE = 128
DELTA_HEAD_DIM = 128
N_DELTA_HEADS = 8
DELTA_WIDTH = N_DELTA_HEADS * DELTA_HEAD_DIM
CONV_WIDTH = 4
CHUNK = 64
D_FF = 4 * D_MODEL
DEPTH = 1
DN_ALPHA = (2.0 * DEPTH) ** 0.25
LN_EPS = 1e-5
RMS_EPS = 1e-6

ATTN_Q = N_ATTN_HEADS * ATTN_HEAD_DIM
ATTN_KV = N_KV_HEADS * ATTN_HEAD_DIM
DELTA_QKV = 3 * DELTA_WIDTH

LANES = 128
SUBLANES = 8
PREV_ROWS = 2 * SUBLANES
VMEM_LIMIT_BYTES = 56 * 1024 * 1024

COL_DQKV = 0
COL_Z = COL_DQKV + DELTA_QKV
COL_Q = COL_Z + DELTA_WIDTH
COL_K = COL_Q + ATTN_Q
COL_V = COL_K + ATTN_KV
PROJ_COLS = COL_V + ATTN_KV

SRC_Q = 0
SRC_DQKV = ATTN_Q + 2 * ATTN_KV
SRC_AB = SRC_DQKV + DELTA_QKV
SRC_Z = SRC_AB + 2 * N_DELTA_HEADS
MXU_COLS = 256


def _dot(a, b):
    return jnp.dot(a, b, preferred_element_type=F32)


def _dot_nt(a, b):
    return lax.dot_general(a, b, (((1,), (1,)), ((), ())), preferred_element_type=F32)


def _dot_tn(a, b):
    return lax.dot_general(a, b, (((0,), (0,)), ((), ())), preferred_element_type=F32)


def _layer_norm(y, g, b):
    mu = jnp.mean(y, axis=-1, keepdims=True)
    yc = y - mu
    var = jnp.mean(yc * yc, axis=-1, keepdims=True)
    return yc * lax.rsqrt(var + LN_EPS) * g + b


IN_TM = 512
IN_CHUNK = 512


def _in_proj_kernel(x_ref, wmain_ref, wz_ref, wab_ref, o_ref, ab_ref):
    xb = x_ref[...].astype(BF16)
    pieces = ((wmain_ref, SRC_DQKV, DELTA_QKV, COL_DQKV),
              (wz_ref, 0, DELTA_WIDTH, COL_Z),
              (wmain_ref, SRC_Q, ATTN_Q + 2 * ATTN_KV, COL_Q))
    for w_ref, src, width, dst in pieces:
        for c in range(0, width, IN_CHUNK):
            o_ref[:, dst + c:dst + c + IN_CHUNK] = _dot(
                xb, w_ref[:, src + c:src + c + IN_CHUNK]).astype(o_ref.dtype)
    ab_ref[...] = _dot(xb, wab_ref[...])[:, 0:LANES]


def _in_proj(x2, w_main, w_z, w_ab):
    t = x2.shape[0]
    const = lambda i: (0, 0)
    resident = pl.Buffered(1)
    return pl.pallas_call(
        _in_proj_kernel,
        out_shape=(jax.ShapeDtypeStruct((t, PROJ_COLS), BF16),
                   jax.ShapeDtypeStruct((t, LANES), F32)),
        grid=(t // IN_TM,),
        in_specs=[
            pl.BlockSpec((IN_TM, D_MODEL), lambda i: (i, 0)),
            pl.BlockSpec(w_main.shape, const, pipeline_mode=resident),
            pl.BlockSpec(w_z.shape, const, pipeline_mode=resident),
            pl.BlockSpec(w_ab.shape, const, pipeline_mode=resident),
        ],
        out_specs=(pl.BlockSpec((IN_TM, PROJ_COLS), lambda i: (i, 0)),
                   pl.BlockSpec((IN_TM, LANES), lambda i: (i, 0))),
        compiler_params=pltpu.CompilerParams(
            dimension_semantics=("arbitrary",),
            vmem_limit_bytes=VMEM_LIMIT_BYTES),
        name="in_proj",
    )(x2, w_main, w_z, w_ab)


def _t5_bucket_table():
    qi = np.arange(ATTN_BLOCK, dtype=np.int64)[:, None]
    kj = np.arange(2 * ATTN_BLOCK, dtype=np.int64)[None, :]
    dist = qi + ATTN_BLOCK - kj
    n = np.maximum(dist, 0)
    max_exact = N_BUCKETS // 2
    nf = np.maximum(n, 1).astype(np.float64)
    large = max_exact + (np.log(nf / max_exact) / math.log(MAX_DISTANCE / max_exact)
                         * (N_BUCKETS - max_exact)).astype(np.int64)
    large = np.minimum(large, N_BUCKETS - 1)
    bucket = np.where(n < max_exact, n, large)
    band = (dist >= 0) & (dist < WINDOW)
    return np.where(band, bucket, -1).astype(np.int32)


def _attn_kernel(bucket_ref, relb_ref, sink_ref, q_ref, kp_ref, kc_ref, vp_ref, vc_ref,
                 o_ref, bias_ref):
    b = pl.program_id(0)
    n = pl.program_id(1)

    @pl.when((b == 0) & (n == 0))
    def _():
        bucket = bucket_ref[...]
        col = lax.broadcasted_iota(jnp.int32, bucket.shape, 1)

        def head_body(h, carry):
            acc = jnp.full(bucket.shape, NEG_INF, F32)
            for bk in range(N_BUCKETS):
                acc = jnp.where(bucket == bk, relb_ref[bk, h], acc)
            bias_ref[0, h] = acc
            bias_ref[1, h] = jnp.where(col < ATTN_BLOCK, NEG_INF, acc)
            return carry

        lax.fori_loop(0, N_ATTN_HEADS, head_body, 0)

    table = jnp.where(n == 0, 1, 0)
    lane = lax.broadcasted_iota(jnp.int32, (2 * ATTN_BLOCK, LANES), 1)
    lo = lane < ATTN_HEAD_DIM

    q = (q_ref[...] * (ATTN_HEAD_DIM ** -0.5)).astype(BF16)
    kcat = jnp.concatenate([kp_ref[...], kc_ref[...]], axis=0).astype(F32)
    vcat = jnp.concatenate([vp_ref[...], vc_ref[...]], axis=0).astype(F32)

    for m in range(N_KV_HEADS // 2):
        kg = kcat[:, m * LANES:(m + 1) * LANES]
        vg = vcat[:, m * LANES:(m + 1) * LANES]
        kg_sw = pltpu.roll(kg, ATTN_HEAD_DIM, axis=1)
        vg_sw = pltpu.roll(vg, ATTN_HEAD_DIM, axis=1)
        for t in range(2):
            h = 2 * m + t
            k_src, k_alt = (kg, kg_sw) if t == 0 else (kg_sw, kg)
            v_src, v_alt = (vg, vg_sw) if t == 0 else (vg_sw, vg)
            k_ab = (jnp.where(lo, k_src, 0.0).astype(BF16), jnp.where(lo, 0.0, k_alt).astype(BF16))
            v_ab = (jnp.where(lo, v_src, 0.0).astype(BF16), jnp.where(lo, 0.0, v_alt).astype(BF16))
            for pp in range(2):
                p = 2 * h + pp
                qp = q[:, p * LANES:(p + 1) * LANES]
                out_pair = jnp.zeros((ATTN_BLOCK, LANES), F32)
                for s in range(2):
                    head = 2 * p + s
                    logits = _dot_nt(qp, k_ab[s]) + bias_ref[table, head]
                    sink = sink_ref[head]
                    mx = jnp.maximum(jnp.max(logits, axis=-1, keepdims=True), sink)
                    pr = jnp.exp(logits - mx)
                    den = jnp.sum(pr, axis=-1, keepdims=True) + jnp.exp(sink - mx)
                    out_pair = out_pair + _dot(pr.astype(BF16), v_ab[s]) * (1.0 / den)
                o_ref[:, p * LANES:(p + 1) * LANES] = out_pair.astype(o_ref.dtype)


def _swa_attention(proj, rel_bias, sinks, batch, seq):
    nb = seq // ATTN_BLOCK
    bucket = jnp.asarray(_t5_bucket_table())
    q_blk = COL_Q // ATTN_Q
    k_blk = COL_K // ATTN_KV
    v_blk = COL_V // ATTN_KV

    def cur(col):
        return lambda b, n: (b * nb + n, col)

    def prev(col):
        return lambda b, n: (b * nb + jnp.maximum(n - 1, 0), col)

    smem = pl.BlockSpec(memory_space=pltpu.SMEM)
    return pl.pallas_call(
        _attn_kernel,
        out_shape=jax.ShapeDtypeStruct((batch * seq, ATTN_Q), BF16),
        grid=(batch, nb),
        in_specs=[
            pl.BlockSpec((ATTN_BLOCK, 2 * ATTN_BLOCK), lambda b, n: (0, 0)),
            smem, smem,
            pl.BlockSpec((ATTN_BLOCK, ATTN_Q), cur(q_blk)),
            pl.BlockSpec((ATTN_BLOCK, ATTN_KV), prev(k_blk)),
            pl.BlockSpec((ATTN_BLOCK, ATTN_KV), cur(k_blk)),
            pl.BlockSpec((ATTN_BLOCK, ATTN_KV), prev(v_blk)),
            pl.BlockSpec((ATTN_BLOCK, ATTN_KV), cur(v_blk)),
        ],
        out_specs=pl.BlockSpec((ATTN_BLOCK, ATTN_Q), lambda b, n: (b * nb + n, 0)),
        scratch_shapes=[pltpu.VMEM((2, N_ATTN_HEADS, ATTN_BLOCK, 2 * ATTN_BLOCK), F32)],
        compiler_params=pltpu.CompilerParams(
            dimension_semantics=("arbitrary", "arbitrary"),
            vmem_limit_bytes=VMEM_LIMIT_BYTES),
        name="swa_attn",
    )(bucket, rel_bias, sinks, proj, proj, proj, proj, proj)


GDN_CHUNKS = 2
GDN_TOK = GDN_CHUNKS * CHUNK


def _gdn_kernel(xprev_ref, x_ref, z_ref, ab_ref, convw_ref, alog_ref, dtb_ref, normw_ref,
                o_ref, state_ref):
    c = pl.program_id(1)

    @pl.when(c == 0)
    def _():
        state_ref[...] = jnp.zeros(state_ref.shape, F32)

    prev = jnp.where(c == 0, 0.0, xprev_ref[...].astype(F32))
    xe = jnp.concatenate([prev, x_ref[...].astype(F32)], axis=0)
    y = convw_ref[CONV_WIDTH - 1:CONV_WIDTH, :] * xe
    for s in range(1, CONV_WIDTH):
        j = CONV_WIDTH - 1 - s
        y = y + convw_ref[j:j + 1, :] * pltpu.roll(xe, s, axis=0)
    y = y[PREV_ROWS:, :]
    y = y * jax.nn.sigmoid(y)

    ab = ab_ref[...]
    g_all = -jnp.exp(alog_ref[...]) * jax.nn.softplus(ab + dtb_ref[...])
    beta_all = jax.nn.sigmoid(ab)
    row = lax.broadcasted_iota(jnp.int32, (CHUNK, CHUNK), 0)
    colm = lax.broadcasted_iota(jnp.int32, (CHUNK, CHUNK), 1)
    tril = row >= colm
    strict = row > colm
    ltri = tril.astype(F32)
    eye = (row == colm).astype(F32)
    normw = normw_ref[...]
    d = DELTA_HEAD_DIM

    items = [(ci, h) for ci in range(GDN_CHUNKS) for h in range(N_DELTA_HEADS)]
    idx = range(len(items))

    def rows(ci):
        return slice(ci * CHUNK, (ci + 1) * CHUNK)

    g_cum = [jnp.dot(ltri, g_all[rows(ci)], precision=lax.Precision.HIGHEST,
                     preferred_element_type=F32) for ci in range(GDN_CHUNKS)]
    g_cum_t = [lax.dot_general(g_all[rows(ci)], ltri, (((0,), (1,)), ((), ())),
                               precision=lax.Precision.HIGHEST, preferred_element_type=F32)
               for ci in range(GDN_CHUNKS)]

    def l2n(t):
        return t * lax.rsqrt(jnp.sum(t * t, axis=-1, keepdims=True) + RMS_EPS)

    q = [l2n(y[rows(ci), h * d:(h + 1) * d]) * (d ** -0.5) for ci, h in items]
    k = [l2n(y[rows(ci), DELTA_WIDTH + h * d:DELTA_WIDTH + (h + 1) * d]) for ci, h in items]
    v = [y[rows(ci), 2 * DELTA_WIDTH + h * d:2 * DELTA_WIDTH + (h + 1) * d] for ci, h in items]
    beta = [beta_all[rows(ci), N_DELTA_HEADS + h:N_DELTA_HEADS + h + 1] for ci, h in items]
    gc = [g_cum[ci][:, h:h + 1] for ci, h in items]
    g_last = [gc[i][CHUNK - 1:CHUNK, :] for i in idx]
    exp_g = [jnp.exp(gc[i]) for i in idx]
    decay = [jnp.exp(jnp.where(tril, gc[i] - g_cum_t[ci][h:h + 1, :], -jnp.inf))
             for i, (ci, h) in enumerate(items)]

    kb = [k[i] * beta[i] for i in idx]
    k_bf = [k[i].astype(BF16) for i in idx]
    kq = [_dot_nt(jnp.concatenate([kb[i], q[i]], axis=0).astype(BF16), k_bf[i]) for i in idx]
    attn = [(kq[i][CHUNK:2 * CHUNK] * decay[i]).astype(BF16) for i in idx]

    nj = [jnp.where(strict, -(kq[i][0:CHUNK] * decay[i]), 0.0) for i in idx]
    pj = [eye + nj[i] for i in idx]
    nj = [_dot(nj[i].astype(BF16), nj[i].astype(BF16)) for i in idx]
    for _ in range(4):
        both = [_dot(jnp.concatenate([pj[i], nj[i]], axis=0).astype(BF16), nj[i].astype(BF16))
                for i in idx]
        pj = [pj[i] + both[i][0:CHUNK] for i in idx]
        nj = [both[i][CHUNK:2 * CHUNK] for i in idx]
    pj = [pj[i] + _dot(pj[i].astype(BF16), nj[i].astype(BF16)) for i in idx]

    sol = [_dot(pj[i].astype(BF16),
                jnp.concatenate([v[i] * beta[i], kb[i] * exp_g[i]], axis=1).astype(BF16))
           for i in idx]
    wq = [jnp.concatenate([sol[i][:, d:2 * d], q[i] * exp_g[i]], axis=0).astype(BF16) for i in idx]
    k_dec = [(k[i] * jnp.exp(g_last[i] - gc[i])).astype(BF16) for i in idx]

    state = [state_ref[h] for h in range(N_DELTA_HEADS)]
    heads = range(N_DELTA_HEADS)
    o = []
    for ci in range(GDN_CHUNKS):
        it = [ci * N_DELTA_HEADS + h for h in heads]
        ws = [_dot(wq[i], state[h].astype(BF16)) for h, i in enumerate(it)]
        v_new = [(sol[i][:, 0:d] - ws[h][0:CHUNK]).astype(BF16) for h, i in enumerate(it)]
        o += [ws[h][CHUNK:2 * CHUNK] + _dot(attn[i], v_new[h]) for h, i in enumerate(it)]
        state = [state[h] * jnp.exp(g_last[i]) + _dot_tn(k_dec[i], v_new[h]) for h, i in enumerate(it)]
    for h in heads:
        state_ref[h] = state[h]

    for i, (ci, h) in enumerate(items):
        oh = o[i] * lax.rsqrt(jnp.mean(o[i] * o[i], axis=-1, keepdims=True) + RMS_EPS) * normw
        zh = z_ref[rows(ci), h * d:(h + 1) * d].astype(F32)
        o_ref[rows(ci), h * d:(h + 1) * d] = (oh * (zh * jax.nn.sigmoid(zh))).astype(o_ref.dtype)


def _gdn(proj, ab, conv_w, alog_row, dtb_row, normw_row, batch, seq):
    ns = seq // GDN_TOK
    prev_per_step = GDN_TOK // PREV_ROWS

    def cur(col):
        return lambda b, c: (b * ns + c, col)

    def prev_rows(b, c):
        return (jnp.maximum((b * ns + c) * prev_per_step - 1, 0), COL_DQKV // DELTA_QKV)

    const = lambda b, c: (0, 0)
    return pl.pallas_call(
        _gdn_kernel,
        out_shape=jax.ShapeDtypeStruct((batch * seq, DELTA_WIDTH), BF16),
        grid=(batch, ns),
        in_specs=[
            pl.BlockSpec((PREV_ROWS, DELTA_QKV), prev_rows),
            pl.BlockSpec((GDN_TOK, DELTA_QKV), cur(COL_DQKV // DELTA_QKV)),
            pl.BlockSpec((GDN_TOK, DELTA_WIDTH), cur(COL_Z // DELTA_WIDTH)),
            pl.BlockSpec((GDN_TOK, LANES), cur(0)),
            pl.BlockSpec((CONV_WIDTH, DELTA_QKV), const),
            pl.BlockSpec((1, LANES), const),
            pl.BlockSpec((1, LANES), const),
            pl.BlockSpec((1, DELTA_HEAD_DIM), const),
        ],
        out_specs=pl.BlockSpec((GDN_TOK, DELTA_WIDTH), lambda b, c: (b * ns + c, 0)),
        scratch_shapes=[
            pltpu.VMEM((N_DELTA_HEADS, DELTA_HEAD_DIM, DELTA_HEAD_DIM), F32),
        ],
        compiler_params=pltpu.CompilerParams(
            dimension_semantics=("arbitrary", "arbitrary"),
            vmem_limit_bytes=VMEM_LIMIT_BYTES),
        name="gdn",
    )(proj, proj, proj, ab, conv_w, alog_row, dtb_row, normw_row)


OUT_TM = 512


def _out_ln1_kernel(x_ref, attn_ref, delta_ref, wo_ref, g_ref, b_ref, o_ref):
    mixed = (_dot(attn_ref[...], wo_ref[0:ATTN_Q, :])
             + _dot(delta_ref[...], wo_ref[ATTN_Q:ATTN_Q + DELTA_WIDTH, :]))
    y = DN_ALPHA * x_ref[...] + mixed
    o_ref[...] = _layer_norm(y, g_ref[...], b_ref[...])


def _out_ln1(x2, attn_out, delta_out, w_o, g, b):
    t = x2.shape[0]
    const = lambda i: (0, 0)
    return pl.pallas_call(
        _out_ln1_kernel,
        out_shape=jax.ShapeDtypeStruct((t, D_MODEL), F32),
        grid=(t // OUT_TM,),
        in_specs=[
            pl.BlockSpec((OUT_TM, D_MODEL), lambda i: (i, 0)),
            pl.BlockSpec((OUT_TM, ATTN_Q), lambda i: (i, 0)),
            pl.BlockSpec((OUT_TM, DELTA_WIDTH), lambda i: (i, 0)),
            pl.BlockSpec((ATTN_Q + DELTA_WIDTH, D_MODEL), const),
            pl.BlockSpec((1, D_MODEL), const),
            pl.BlockSpec((1, D_MODEL), const),
        ],
        out_specs=pl.BlockSpec((OUT_TM, D_MODEL), lambda i: (i, 0)),
        compiler_params=pltpu.CompilerParams(
            dimension_semantics=("arbitrary",),
            vmem_limit_bytes=VMEM_LIMIT_BYTES),
        name="out_ln1",
    )(x2, attn_out, delta_out, w_o, g, b)


MLP_TM = 1024
MLP_TF = 512


def _mlp_ln2_kernel(x_ref, wup_ref, wdn_ref, g_ref, b_ref, o_ref, xb_ref):
    j = pl.program_id(1)

    @pl.when(j == 0)
    def _():
        x = x_ref[...]
        xb_ref[...] = x.astype(BF16)
        o_ref[...] = DN_ALPHA * x

    a = jnp.maximum(_dot(xb_ref[...], wup_ref[...]), 0.0)
    o_ref[...] += _dot((a * a).astype(BF16), wdn_ref[...])

    @pl.when(j == pl.num_programs(1) - 1)
    def _():
        o_ref[...] = _layer_norm(o_ref[...], g_ref[...], b_ref[...])


def _mlp_ln2(x1, w_up, w_down, g, b):
    t = x1.shape[0]
    const = lambda i, j: (0, 0)
    return pl.pallas_call(
        _mlp_ln2_kernel,
        out_shape=jax.ShapeDtypeStruct((t, D_MODEL), F32),
        grid=(t // MLP_TM, D_FF // MLP_TF),
        in_specs=[
            pl.BlockSpec((MLP_TM, D_MODEL), lambda i, j: (i, 0)),
            pl.BlockSpec((D_MODEL, MLP_TF), lambda i, j: (0, j)),
            pl.BlockSpec((MLP_TF, D_MODEL), lambda i, j: (j, 0)),
            pl.BlockSpec((1, D_MODEL), const),
            pl.BlockSpec((1, D_MODEL), const),
        ],
        out_specs=pl.BlockSpec((MLP_TM, D_MODEL), lambda i, j: (i, 0)),
        scratch_shapes=[pltpu.VMEM((MLP_TM, D_MODEL), BF16)],
        compiler_params=pltpu.CompilerParams(
            dimension_semantics=("arbitrary", "arbitrary"),
            vmem_limit_bytes=VMEM_LIMIT_BYTES),
        name="mlp_ln2",
    )(x1, w_up, w_down, g, b)


def _split_w_in(w_in):
    w_main = w_in[:, :SRC_AB].astype(BF16)
    w_z = w_in[:, SRC_Z:].astype(BF16)
    w_ab = jnp.pad(w_in[:, SRC_AB:SRC_Z], ((0, 0), (0, MXU_COLS - 2 * N_DELTA_HEADS))).astype(BF16)
    return w_main, w_z, w_ab


def _lane_row(v):
    return jnp.zeros((1, LANES), F32).at[0, :v.shape[0]].set(v.astype(F32))


def kernel(x, w_in, conv_w, a_log, dt_bias, delta_norm_w, attn_sinks, rel_bias, w_o, ln1_g, ln1_b,
           w_up, w_down, ln2_g, ln2_b):
    batch, seq, d = x.shape
    assert d == D_MODEL and seq % ATTN_BLOCK == 0 and (batch * seq) % IN_TM == 0
    assert w_in.shape[0] == DEPTH
    x2 = x.reshape(batch * seq, d)
    for l in range(DEPTH):
        proj, ab = _in_proj(x2, *_split_w_in(w_in[l]))
        attn_out = _swa_attention(proj, rel_bias.astype(F32), attn_sinks[l].astype(F32), batch, seq)
        delta_out = _gdn(proj, ab, conv_w[l].reshape(CONV_WIDTH, DELTA_QKV).astype(F32),
                         _lane_row(a_log[l]), _lane_row(dt_bias[l]),
                         delta_norm_w[l].reshape(1, DELTA_HEAD_DIM).astype(F32), batch, seq)
        x1 = _out_ln1(x2, attn_out, delta_out, w_o[l].astype(BF16),
                      ln1_g[l].reshape(1, d), ln1_b[l].reshape(1, d))
        x2 = _mlp_ln2(x1, w_up[l].astype(BF16), w_down[l].astype(BF16),
                      ln2_g[l].reshape(1, d), ln2_b[l].reshape(1, d))
    return x2.reshape(batch, seq, d)
```

```python
import functools
import math

import jax
import jax.numpy as jnp
import numpy as np
from jax import lax
from jax.experimental import pallas as pl
from jax.experimental.pallas import tpu as pltpu

F32 = jnp.float32
BF16 = jnp.bfloat16

D_MODEL = 2048
ATTN_HEAD_DIM = 64
N_ATTN_HEADS = 16
N_KV_HEADS = 4
ATTN_BLOCK = 128
WINDOW = 128
NEG_INF = -1e30
N_BUCKETS = 32
MAX_DISTANCE = 128
DELTA_HEAD_DIM = 128
N_DELTA_HEADS = 8
DELTA_WIDTH = N_DELTA_HEADS * DELTA_HEAD_DIM
CONV_WIDTH = 4
CHUNK = 64
D_FF = 4 * D_MODEL
DEPTH = 1
DN_ALPHA = (2.0 * DEPTH) ** 0.25
LN_EPS = 1e-5
RMS_EPS = 1e-6

ATTN_Q = N_ATTN_HEADS * ATTN_HEAD_DIM
ATTN_KV = N_KV_HEADS * ATTN_HEAD_DIM
DELTA_QKV = 3 * DELTA_WIDTH

LANES = 128
SUBLANES = 8
PREV_ROWS = 2 * SUBLANES
VMEM_LIMIT_BYTES = 56 * 1024 * 1024

COL_DQKV = 0
COL_Z = COL_DQKV + DELTA_QKV
COL_Q = COL_Z + DELTA_WIDTH
COL_K = COL_Q + ATTN_Q
COL_V = COL_K + ATTN_KV
PROJ_COLS = COL_V + ATTN_KV

SRC_Q = 0
SRC_DQKV = ATTN_Q + 2 * ATTN_KV
SRC_AB = SRC_DQKV + DELTA_QKV
SRC_Z = SRC_AB + 2 * N_DELTA_HEADS
MXU_COLS = 256


def _dot(a, b):
    return jnp.dot(a, b, preferred_element_type=F32)


def _dot_nt(a, b):
    return lax.dot_general(a, b, (((1,), (1,)), ((), ())), preferred_element_type=F32)


def _dot_tn(a, b):
    return lax.dot_general(a, b, (((0,), (0,)), ((), ())), preferred_element_type=F32)


def _layer_norm(y, g, b):
    mu = jnp.mean(y, axis=-1, keepdims=True)
    yc = y - mu
    var = jnp.mean(yc * yc, axis=-1, keepdims=True)
    return yc * lax.rsqrt(var + LN_EPS) * g + b


IN_TM = 512
IN_CHUNK = 512


def _in_proj_kernel(x_ref, wmain_ref, wz_ref, wab_ref, o_ref, ab_ref):
    xb = x_ref[...].astype(BF16)
    pieces = ((wmain_ref, SRC_DQKV, DELTA_QKV, COL_DQKV),
              (wz_ref, 0, DELTA_WIDTH, COL_Z),
              (wmain_ref, SRC_Q, ATTN_Q + 2 * ATTN_KV, COL_Q))
    for w_ref, src, width, dst in pieces:
        for c in range(0, width, IN_CHUNK):
            o_ref[:, dst + c:dst + c + IN_CHUNK] = _dot(
                xb, w_ref[:, src + c:src + c + IN_CHUNK]).astype(o_ref.dtype)
    ab_ref[...] = _dot(xb, wab_ref[...])[:, 0:LANES]


def _in_proj(x2, w_main, w_z, w_ab):
    t = x2.shape[0]
    const = lambda i: (0, 0)
    resident = pl.Buffered(1)
    return pl.pallas_call(
        _in_proj_kernel,
        out_shape=(jax.ShapeDtypeStruct((t, PROJ_COLS), BF16),
                   jax.ShapeDtypeStruct((t, LANES), F32)),
        grid=(t // IN_TM,),
        in_specs=[
            pl.BlockSpec((IN_TM, D_MODEL), lambda i: (i, 0)),
            pl.BlockSpec(w_main.shape, const, pipeline_mode=resident),
            pl.BlockSpec(w_z.shape, const, pipeline_mode=resident),
            pl.BlockSpec(w_ab.shape, const, pipeline_mode=resident),
        ],
        out_specs=(pl.BlockSpec((IN_TM, PROJ_COLS), lambda i: (i, 0)),
                   pl.BlockSpec((IN_TM, LANES), lambda i: (i, 0))),
        compiler_params=pltpu.CompilerParams(
            dimension_semantics=("arbitrary",),
            vmem_limit_bytes=VMEM_LIMIT_BYTES),
        name="in_proj",
    )(x2, w_main, w_z, w_ab)


def _t5_bucket_table():
    qi = np.arange(ATTN_BLOCK, dtype=np.int64)[:, None]
    kj = np.arange(2 * ATTN_BLOCK, dtype=np.int64)[None, :]
    dist = qi + ATTN_BLOCK - kj
    n = np.maximum(dist, 0)
    max_exact = N_BUCKETS // 2
    nf = np.maximum(n, 1).astype(np.float64)
    large = max_exact + (np.log(nf / max_exact) / math.log(MAX_DISTANCE / max_exact)
                         * (N_BUCKETS - max_exact)).astype(np.int64)
    large = np.minimum(large, N_BUCKETS - 1)
    bucket = np.where(n < max_exact, n, large)
    band = (dist >= 0) & (dist < WINDOW)
    return np.where(band, bucket, -1).astype(np.int32)


def _attn_kernel(bucket_ref, relb_ref, sink_ref, q_ref, kp_ref, kc_ref, vp_ref, vc_ref,
                 o_ref, bias_ref):
    b = pl.program_id(0)
    n = pl.program_id(1)

    @pl.when((b == 0) & (n == 0))
    def _():
        bucket = bucket_ref[...]
        col = lax.broadcasted_iota(jnp.int32, bucket.shape, 1)

        def head_body(h, carry):
            acc = jnp.full(bucket.shape, NEG_INF, F32)
            for bk in range(N_BUCKETS):
                acc = jnp.where(bucket == bk, relb_ref[bk, h], acc)
            bias_ref[0, h] = acc
            bias_ref[1, h] = jnp.where(col < ATTN_BLOCK, NEG_INF, acc)
            return carry

        lax.fori_loop(0, N_ATTN_HEADS, head_body, 0)

    table = jnp.where(n == 0, 1, 0)
    lane = lax.broadcasted_iota(jnp.int32, (2 * ATTN_BLOCK, LANES), 1)
    lo = lane < ATTN_HEAD_DIM

    q = (q_ref[...] * (ATTN_HEAD_DIM ** -0.5)).astype(BF16)
    kcat = jnp.concatenate([kp_ref[...], kc_ref[...]], axis=0).astype(F32)
    vcat = jnp.concatenate([vp_ref[...], vc_ref[...]], axis=0).astype(F32)

    for m in range(N_KV_HEADS // 2):
        kg = kcat[:, m * LANES:(m + 1) * LANES]
        vg = vcat[:, m * LANES:(m + 1) * LANES]
        kg_sw = pltpu.roll(kg, ATTN_HEAD_DIM, axis=1)
        vg_sw = pltpu.roll(vg, ATTN_HEAD_DIM, axis=1)
        for t in range(2):
            h = 2 * m + t
            k_src, k_alt = (kg, kg_sw) if t == 0 else (kg_sw, kg)
            v_src, v_alt = (vg, vg_sw) if t == 0 else (vg_sw, vg)
            k_ab = (jnp.where(lo, k_src, 0.0).astype(BF16), jnp.where(lo, 0.0, k_alt).astype(BF16))
            v_ab = (jnp.where(lo, v_src, 0.0).astype(BF16), jnp.where(lo, 0.0, v_alt).astype(BF16))
            for pp in range(2):
                p = 2 * h + pp
                qp = q[:, p * LANES:(p + 1) * LANES]
                out_pair = jnp.zeros((ATTN_BLOCK, LANES), F32)
                for s in range(2):
                    head = 2 * p + s
                    logits = _dot_nt(qp, k_ab[s]) + bias_ref[table, head]
                    sink = sink_ref[head]
                    mx = jnp.maximum(jnp.max(logits, axis=-1, keepdims=True), sink)
                    pr = jnp.exp(logits - mx)
                    den = jnp.sum(pr, axis=-1, keepdims=True) + jnp.exp(sink - mx)
                    out_pair = out_pair + _dot(pr.astype(BF16), v_ab[s]) * (1.0 / den)
                o_ref[:, p * LANES:(p + 1) * LANES] = out_pair.astype(o_ref.dtype)


def _swa_attention(proj, rel_bias, sinks, batch, seq):
    nb = seq // ATTN_BLOCK
    bucket = jnp.asarray(_t5_bucket_table())
    q_blk = COL_Q // ATTN_Q
    k_blk = COL_K // ATTN_KV
    v_blk = COL_V // ATTN_KV

    def cur(col):
        return lambda b, n: (b * nb + n, col)

    def prev(col):
        return lambda b, n: (b * nb + jnp.maximum(n - 1, 0), col)

    smem = pl.BlockSpec(memory_space=pltpu.SMEM)
    return pl.pallas_call(
        _attn_kernel,
        out_shape=jax.ShapeDtypeStruct((batch * seq, ATTN_Q), BF16),
        grid=(batch, nb),
        in_specs=[
            pl.BlockSpec((ATTN_BLOCK, 2 * ATTN_BLOCK), lambda b, n: (0, 0)),
            smem, smem,
            pl.BlockSpec((ATTN_BLOCK, ATTN_Q), cur(q_blk)),
            pl.BlockSpec((ATTN_BLOCK, ATTN_KV), prev(k_blk)),
            pl.BlockSpec((ATTN_BLOCK, ATTN_KV), cur(k_blk)),
            pl.BlockSpec((ATTN_BLOCK, ATTN_KV), prev(v_blk)),
            pl.BlockSpec((ATTN_BLOCK, ATTN_KV), cur(v_blk)),
        ],
        out_specs=pl.BlockSpec((ATTN_BLOCK, ATTN_Q), lambda b, n: (b * nb + n, 0)),
        scratch_shapes=[pltpu.VMEM((2, N_ATTN_HEADS, ATTN_BLOCK, 2 * ATTN_BLOCK), F32)],
        compiler_params=pltpu.CompilerParams(
            dimension_semantics=("arbitrary", "arbitrary"),
            vmem_limit_bytes=VMEM_LIMIT_BYTES),
        name="swa_attn",
    )(bucket, rel_bias, sinks, proj, proj, proj, proj, proj)


GDN_CHUNKS = 4
GDN_TOK = GDN_CHUNKS * CHUNK


def _gdn_kernel(xprev_ref, x_ref, z_ref, ab_ref, convw_ref, alog_ref, dtb_ref, normw_ref,
                o_ref, state_ref):
    c = pl.program_id(1)

    @pl.when(c == 0)
    def _():
        state_ref[...] = jnp.zeros(state_ref.shape, F32)

    prev = jnp.where(c == 0, 0.0, xprev_ref[...].astype(F32))
    xe = jnp.concatenate([prev[SUBLANES:], x_ref[...].astype(F32)], axis=0)

    def shift_rows(t, s):
        tiles = t.reshape(t.shape[0] // SUBLANES, SUBLANES, t.shape[1])
        rot = pltpu.roll(tiles, s, axis=1)
        sub = lax.broadcasted_iota(jnp.int32, tiles.shape, 1)
        return jnp.where(sub < s, pltpu.roll(rot, 1, axis=0), rot).reshape(t.shape)

    w = [convw_ref[j:j + 1, :] for j in range(CONV_WIDTH)]
    x2 = shift_rows(xe, 2)
    y = w[3] * xe + w[1] * x2 + shift_rows(w[2] * xe + w[0] * x2, 1)
    y = y[SUBLANES:, :]
    y = y * jax.nn.sigmoid(y)

    ab = ab_ref[...]
    g_all = -jnp.exp(alog_ref[...]) * jax.nn.softplus(ab + dtb_ref[...])
    beta_all = jax.nn.sigmoid(ab)
    row = lax.broadcasted_iota(jnp.int32, (CHUNK, CHUNK), 0)
    colm = lax.broadcasted_iota(jnp.int32, (CHUNK, CHUNK), 1)
    tril = row >= colm
    strict = row > colm
    ltri = tril.astype(F32)
    eye = (row == colm).astype(F32)
    normw = normw_ref[...]
    d = DELTA_HEAD_DIM

    items = [(ci, h) for ci in range(GDN_CHUNKS) for h in range(N_DELTA_HEADS)]
    idx = range(len(items))

    def rows(ci):
        return slice(ci * CHUNK, (ci + 1) * CHUNK)

    g_cum = [jnp.dot(ltri, g_all[rows(ci)], precision=lax.Precision.HIGHEST,
                     preferred_element_type=F32) for ci in range(GDN_CHUNKS)]
    g_cum_t = [lax.dot_general(g_all[rows(ci)], ltri, (((0,), (1,)), ((), ())),
                               precision=lax.Precision.HIGHEST, preferred_element_type=F32)
               for ci in range(GDN_CHUNKS)]

    def l2n(t):
        return t * lax.rsqrt(jnp.sum(t * t, axis=-1, keepdims=True) + RMS_EPS)

    q = [l2n(y[rows(ci), h * d:(h + 1) * d]) * (d ** -0.5) for ci, h in items]
    k = [l2n(y[rows(ci), DELTA_WIDTH + h * d:DELTA_WIDTH + (h + 1) * d]) for ci, h in items]
    v = [y[rows(ci), 2 * DELTA_WIDTH + h * d:2 * DELTA_WIDTH + (h + 1) * d] for ci, h in items]
    beta = [beta_all[rows(ci), N_DELTA_HEADS + h:N_DELTA_HEADS + h + 1] for ci, h in items]
    gc = [g_cum[ci][:, h:h + 1] for ci, h in items]
    g_last = [gc[i][CHUNK - 1:CHUNK, :] for i in idx]
    exp_g = [jnp.exp(gc[i]) for i in idx]
    decay = [jnp.exp(jnp.where(tril, gc[i] - g_cum_t[ci][h:h + 1, :], -jnp.inf))
             for i, (ci, h) in enumerate(items)]

    kb = [k[i] * beta[i] for i in idx]
    k_bf = [k[i].astype(BF16) for i in idx]
    kq = [_dot_nt(jnp.concatenate([kb[i], q[i]], axis=0).astype(BF16), k_bf[i]) for i in idx]
    attn = [(kq[i][CHUNK:2 * CHUNK] * decay[i]).astype(BF16) for i in idx]

    nj = [jnp.where(strict, -(kq[i][0:CHUNK] * decay[i]), 0.0) for i in idx]
    pj = [eye + nj[i] for i in idx]
    nj = [_dot(nj[i].astype(BF16), nj[i].astype(BF16)) for i in idx]
    for _ in range(4):
        both = [_dot(jnp.concatenate([pj[i], nj[i]], axis=0).astype(BF16), nj[i].astype(BF16))
                for i in idx]
        pj = [pj[i] + both[i][0:CHUNK] for i in idx]
        nj = [both[i][CHUNK:2 * CHUNK] for i in idx]
    pj = [pj[i] + _dot(pj[i].astype(BF16), nj[i].astype(BF16)) for i in idx]

    sol = [_dot(pj[i].astype(BF16),
                jnp.concatenate([v[i] * beta[i], kb[i] * exp_g[i]], axis=1).astype(BF16))
           for i in idx]
    wq = [jnp.concatenate([sol[i][:, d:2 * d], q[i] * exp_g[i]], axis=0).astype(BF16) for i in idx]
    k_dec = [(k[i] * jnp.exp(g_last[i] - gc[i])).astype(BF16) for i in idx]

    state = [state_ref[h] for h in range(N_DELTA_HEADS)]
    heads = range(N_DELTA_HEADS)
    o = []
    for ci in range(GDN_CHUNKS):
        it = [ci * N_DELTA_HEADS + h for h in heads]
        ws = [_dot(wq[i], state[h].astype(BF16)) for h, i in enumerate(it)]
        v_new = [(sol[i][:, 0:d] - ws[h][0:CHUNK]).astype(BF16) for h, i in enumerate(it)]
        o += [ws[h][CHUNK:2 * CHUNK] + _dot(attn[i], v_new[h]) for h, i in enumerate(it)]
        state = [state[h] * jnp.exp(g_last[i]) + _dot_tn(k_dec[i], v_new[h]) for h, i in enumerate(it)]
    for h in heads:
        state_ref[h] = state[h]

    for i, (ci, h) in enumerate(items):
        oh = o[i] * lax.rsqrt(jnp.mean(o[i] * o[i], axis=-1, keepdims=True) + RMS_EPS) * normw
        zh = z_ref[rows(ci), h * d:(h + 1) * d].astype(F32)
        o_ref[rows(ci), h * d:(h + 1) * d] = (oh * (zh * jax.nn.sigmoid(zh))).astype(o_ref.dtype)


def _gdn(proj, ab, conv_w, alog_row, dtb_row, normw_row, batch, seq):
    ns = seq // GDN_TOK
    prev_per_step = GDN_TOK // PREV_ROWS

    def cur(col):
        return lambda b, c: (b * ns + c, col)

    def prev_rows(b, c):
        return (jnp.maximum((b * ns + c) * prev_per_step - 1, 0), COL_DQKV // DELTA_QKV)

    const = lambda b, c: (0, 0)
    return pl.pallas_call(
        _gdn_kernel,
        out_shape=jax.ShapeDtypeStruct((batch * seq, DELTA_WIDTH), BF16),
        grid=(batch, ns),
        in_specs=[
            pl.BlockSpec((PREV_ROWS, DELTA_QKV), prev_rows),
            pl.BlockSpec((GDN_TOK, DELTA_QKV), cur(COL_DQKV // DELTA_QKV)),
            pl.BlockSpec((GDN_TOK, DELTA_WIDTH), cur(COL_Z // DELTA_WIDTH)),
            pl.BlockSpec((GDN_TOK, LANES), cur(0)),
            pl.BlockSpec((CONV_WIDTH, DELTA_QKV), const),
            pl.BlockSpec((1, LANES), const),
            pl.BlockSpec((1, LANES), const),
            pl.BlockSpec((1, DELTA_HEAD_DIM), const),
        ],
        out_specs=pl.BlockSpec((GDN_TOK, DELTA_WIDTH), lambda b, c: (b * ns + c, 0)),
        scratch_shapes=[
            pltpu.VMEM((N_DELTA_HEADS, DELTA_HEAD_DIM, DELTA_HEAD_DIM), F32),
        ],
        compiler_params=pltpu.CompilerParams(
            dimension_semantics=("arbitrary", "arbitrary"),
            vmem_limit_bytes=VMEM_LIMIT_BYTES),
        name="gdn",
    )(proj, proj, proj, ab, conv_w, alog_row, dtb_row, normw_row)


OUT_TM = 512


def _out_ln1_kernel(x_ref, attn_ref, delta_ref, wo_ref, g_ref, b_ref, o_ref):
    mixed = (_dot(attn_ref[...], wo_ref[0:ATTN_Q, :])
             + _dot(delta_ref[...], wo_ref[ATTN_Q:ATTN_Q + DELTA_WIDTH, :]))
    y = DN_ALPHA * x_ref[...] + mixed
    o_ref[...] = _layer_norm(y, g_ref[...], b_ref[...])


def _out_ln1(x2, attn_out, delta_out, w_o, g, b):
    t = x2.shape[0]
    const = lambda i: (0, 0)
    return pl.pallas_call(
        _out_ln1_kernel,
        out_shape=jax.ShapeDtypeStruct((t, D_MODEL), F32),
        grid=(t // OUT_TM,),
        in_specs=[
            pl.BlockSpec((OUT_TM, D_MODEL), lambda i: (i, 0)),
            pl.BlockSpec((OUT_TM, ATTN_Q), lambda i: (i, 0)),
            pl.BlockSpec((OUT_TM, DELTA_WIDTH), lambda i: (i, 0)),
            pl.BlockSpec((ATTN_Q + DELTA_WIDTH, D_MODEL), const),
            pl.BlockSpec((1, D_MODEL), const),
            pl.BlockSpec((1, D_MODEL), const),
        ],
        out_specs=pl.BlockSpec((OUT_TM, D_MODEL), lambda i: (i, 0)),
        compiler_params=pltpu.CompilerParams(
            dimension_semantics=("arbitrary",),
            vmem_limit_bytes=VMEM_LIMIT_BYTES),
        name="out_ln1",
    )(x2, attn_out, delta_out, w_o, g, b)


MLP_TM = 1024
MLP_TF = 512


def _mlp_ln2_kernel(x_ref, wup_ref, wdn_ref, g_ref, b_ref, o_ref, xb_ref):
    j = pl.program_id(1)

    @pl.when(j == 0)
    def _():
        x = x_ref[...]
        xb_ref[...] = x.astype(BF16)
        o_ref[...] = DN_ALPHA * x

    a = jnp.maximum(_dot(xb_ref[...], wup_ref[...]), 0.0)
    o_ref[...] += _dot((a * a).astype(BF16), wdn_ref[...])

    @pl.when(j == pl.num_programs(1) - 1)
    def _():
        o_ref[...] = _layer_norm(o_ref[...], g_ref[...], b_ref[...])


def _mlp_ln2(x1, w_up, w_down, g, b):
    t = x1.shape[0]
    const = lambda i, j: (0, 0)
    return pl.pallas_call(
        _mlp_ln2_kernel,
        out_shape=jax.ShapeDtypeStruct((t, D_MODEL), F32),
        grid=(t // MLP_TM, D_FF // MLP_TF),
        in_specs=[
            pl.BlockSpec((MLP_TM, D_MODEL), lambda i, j: (i, 0)),
            pl.BlockSpec((D_MODEL, MLP_TF), lambda i, j: (0, j)),
            pl.BlockSpec((MLP_TF, D_MODEL), lambda i, j: (j, 0)),
            pl.BlockSpec((1, D_MODEL), const),
            pl.BlockSpec((1, D_MODEL), const),
        ],
        out_specs=pl.BlockSpec((MLP_TM, D_MODEL), lambda i, j: (i, 0)),
        scratch_shapes=[pltpu.VMEM((MLP_TM, D_MODEL), BF16)],
        compiler_params=pltpu.CompilerParams(
            dimension_semantics=("arbitrary", "arbitrary"),
            vmem_limit_bytes=VMEM_LIMIT_BYTES),
        name="mlp_ln2",
    )(x1, w_up, w_down, g, b)


def _split_w_in(w_in):
    w_main = w_in[:, :SRC_AB].astype(BF16)
    w_z = w_in[:, SRC_Z:].astype(BF16)
    w_ab = jnp.pad(w_in[:, SRC_AB:SRC_Z], ((0, 0), (0, MXU_COLS - 2 * N_DELTA_HEADS))).astype(BF16)
    return w_main, w_z, w_ab


def _lane_row(v):
    return jnp.zeros((1, LANES), F32).at[0, :v.shape[0]].set(v.astype(F32))


def kernel(x, w_in, conv_w, a_log, dt_bias, delta_norm_w, attn_sinks, rel_bias, w_o, ln1_g, ln1_b,
           w_up, w_down, ln2_g, ln2_b):
    batch, seq, d = x.shape
    assert d == D_MODEL and seq % ATTN_BLOCK == 0 and (batch * seq) % IN_TM == 0
    assert w_in.shape[0] == DEPTH
    x2 = x.reshape(batch * seq, d)
    for l in range(DEPTH):
        proj, ab = _in_proj(x2, *_split_w_in(w_in[l]))
        attn_out = _swa_attention(proj, rel_bias.astype(F32), attn_sinks[l].astype(F32), batch, seq)
        delta_out = _gdn(proj, ab, conv_w[l].reshape(CONV_WIDTH, DELTA_QKV).astype(F32),
                         _lane_row(a_log[l]), _lane_row(dt_bias[l]),
                         delta_norm_w[l].reshape(1, DELTA_HEAD_DIM).astype(F32), batch, seq)
        x1 = _out_ln1(x2, attn_out, delta_out, w_o[l].astype(BF16),
                      ln1_g[l].reshape(1, d), ln1_b[l].reshape(1, d))
        x2 = _mlp_ln2(x1, w_up[l].astype(BF16), w_down[l].astype(BF16),
                      ln2_g[l].reshape(1, d), ln2_b[l].reshape(1, d))
    return x2.reshape(batch, seq, d)
```

```python
import functools
import math

import jax
import jax.numpy as jnp
import numpy as np
from jax import lax
from jax.experimental import pallas as pl
from jax.experimental.pallas import tpu as pltpu

F32 = jnp.float32
BF16 = jnp.bfloat16

D_MODEL = 2048
ATTN_HEAD_DIM = 64
N_ATTN_HEADS = 16
N_KV_HEADS = 4
ATTN_BLOCK = 128
WINDOW = 128
NEG_INF = -1e30
N_BUCKETS = 32
MAX_DISTANCE = 128
DELTA_HEAD_DIM = 128
N_DELTA_HEADS = 8
DELTA_WIDTH = N_DELTA_HEADS * DELTA_HEAD_DIM
CONV_WIDTH = 4
CHUNK = 64
D_FF = 4 * D_MODEL
DEPTH = 1
DN_ALPHA = (2.0 * DEPTH) ** 0.25
LN_EPS = 1e-5
RMS_EPS = 1e-6

ATTN_Q = N_ATTN_HEADS * ATTN_HEAD_DIM
ATTN_KV = N_KV_HEADS * ATTN_HEAD_DIM
DELTA_QKV = 3 * DELTA_WIDTH

LANES = 128
SUBLANES = 8
PREV_ROWS = 2 * SUBLANES
VMEM_LIMIT_BYTES = 56 * 1024 * 1024

COL_DQKV = 0
COL_Z = COL_DQKV + DELTA_QKV
COL_Q = COL_Z + DELTA_WIDTH
COL_K = COL_Q + ATTN_Q
COL_V = COL_K + ATTN_KV
PROJ_COLS = COL_V + ATTN_KV

SRC_Q = 0
SRC_DQKV = ATTN_Q + 2 * ATTN_KV
SRC_AB = SRC_DQKV + DELTA_QKV
SRC_Z = SRC_AB + 2 * N_DELTA_HEADS
MXU_COLS = 256


def _dot(a, b):
    return jnp.dot(a, b, preferred_element_type=F32)


def _dot_nt(a, b):
    return lax.dot_general(a, b, (((1,), (1,)), ((), ())), preferred_element_type=F32)


def _dot_tn(a, b):
    return lax.dot_general(a, b, (((0,), (0,)), ((), ())), preferred_element_type=F32)


def _layer_norm(y, g, b):
    mu = jnp.mean(y, axis=-1, keepdims=True)
    yc = y - mu
    var = jnp.mean(yc * yc, axis=-1, keepdims=True)
    return yc * lax.rsqrt(var + LN_EPS) * g + b


IN_TM = 512
IN_CHUNK = 512


def _in_proj_kernel(x_ref, wmain_ref, wz_ref, wab_ref, o_ref, ab_ref):
    xb = x_ref[...].astype(BF16)
    pieces = ((wmain_ref, SRC_DQKV, DELTA_QKV, COL_DQKV),
              (wz_ref, 0, DELTA_WIDTH, COL_Z),
              (wmain_ref, SRC_Q, ATTN_Q + 2 * ATTN_KV, COL_Q))
    for w_ref, src, width, dst in pieces:
        for c in range(0, width, IN_CHUNK):
            o_ref[:, dst + c:dst + c + IN_CHUNK] = _dot(
                xb, w_ref[:, src + c:src + c + IN_CHUNK]).astype(o_ref.dtype)
    ab_ref[...] = _dot(xb, wab_ref[...])[:, 0:LANES]


def _in_proj(x2, w_main, w_z, w_ab):
    t = x2.shape[0]
    const = lambda i: (0, 0)
    resident = pl.Buffered(1)
    return pl.pallas_call(
        _in_proj_kernel,
        out_shape=(jax.ShapeDtypeStruct((t, PROJ_COLS), BF16),
                   jax.ShapeDtypeStruct((t, LANES), F32)),
        grid=(t // IN_TM,),
        in_specs=[
            pl.BlockSpec((IN_TM, D_MODEL), lambda i: (i, 0)),
            pl.BlockSpec(w_main.shape, const, pipeline_mode=resident),
            pl.BlockSpec(w_z.shape, const, pipeline_mode=resident),
            pl.BlockSpec(w_ab.shape, const, pipeline_mode=resident),
        ],
        out_specs=(pl.BlockSpec((IN_TM, PROJ_COLS), lambda i: (i, 0)),
                   pl.BlockSpec((IN_TM, LANES), lambda i: (i, 0))),
        compiler_params=pltpu.CompilerParams(
            dimension_semantics=("arbitrary",),
            vmem_limit_bytes=VMEM_LIMIT_BYTES),
        name="in_proj",
    )(x2, w_main, w_z, w_ab)


def _t5_bucket_table():
    qi = np.arange(ATTN_BLOCK, dtype=np.int64)[:, None]
    kj = np.arange(ATTN_BLOCK, dtype=np.int64)[None, :]
    dist = np.where(kj <= qi, qi - kj, qi + ATTN_BLOCK - kj)
    assert dist.min() >= 0 and dist.max() < WINDOW
    max_exact = N_BUCKETS // 2
    nf = np.maximum(dist, 1).astype(np.float64)
    large = max_exact + (np.log(nf / max_exact) / math.log(MAX_DISTANCE / max_exact)
                         * (N_BUCKETS - max_exact)).astype(np.int64)
    large = np.minimum(large, N_BUCKETS - 1)
    return np.where(dist < max_exact, dist, large).astype(np.int32)


def _attn_kernel(bucket_ref, relb_ref, sink_ref, q_ref, kp_ref, kc_ref, vp_ref, vc_ref,
                 o_ref, bias_ref):
    b = pl.program_id(0)
    n = pl.program_id(1)
    row = lax.broadcasted_iota(jnp.int32, (ATTN_BLOCK, ATTN_BLOCK), 0)
    col = lax.broadcasted_iota(jnp.int32, (ATTN_BLOCK, ATTN_BLOCK), 1)
    in_cur = col <= row

    @pl.when((b == 0) & (n == 0))
    def _():
        bucket = bucket_ref[...]

        def head_body(h, carry):
            acc = jnp.zeros(bucket.shape, F32)
            for bk in range(N_BUCKETS):
                acc = jnp.where(bucket == bk, relb_ref[bk, h], acc)
            bias_ref[0, h] = acc
            bias_ref[1, h] = jnp.where(in_cur, acc, NEG_INF)
            return carry

        lax.fori_loop(0, N_ATTN_HEADS, head_body, 0)

    table = jnp.where(n == 0, 1, 0)
    lane = lax.broadcasted_iota(jnp.int32, (2 * ATTN_BLOCK, LANES), 1)
    lo = lane < ATTN_HEAD_DIM

    q = q_ref[...] * (ATTN_HEAD_DIM ** -0.5)
    kcat = jnp.concatenate([kp_ref[...], kc_ref[...]], axis=0).astype(F32)
    vcat = jnp.concatenate([vp_ref[...], vc_ref[...]], axis=0).astype(F32)

    k_ab, v_ab = [], []
    for m in range(N_KV_HEADS // 2):
        kg = kcat[:, m * LANES:(m + 1) * LANES]
        vg = vcat[:, m * LANES:(m + 1) * LANES]
        kg_sw = pltpu.roll(kg, ATTN_HEAD_DIM, axis=1)
        vg_sw = pltpu.roll(vg, ATTN_HEAD_DIM, axis=1)
        for t in range(2):
            k_src, k_alt = (kg, kg_sw) if t == 0 else (kg_sw, kg)
            v_src, v_alt = (vg, vg_sw) if t == 0 else (vg_sw, vg)
            k_ab.append((jnp.where(lo, k_src, 0.0).astype(BF16), jnp.where(lo, 0.0, k_alt).astype(BF16)))
            v_ab.append((jnp.where(lo, v_src, 0.0).astype(BF16), jnp.where(lo, 0.0, v_alt).astype(BF16)))

    heads = range(N_ATTN_HEADS)
    kv_of = [hd // (N_ATTN_HEADS // N_KV_HEADS) for hd in heads]
    scores = [_dot_nt(q[:, (hd // 2) * LANES:(hd // 2 + 1) * LANES], k_ab[kv_of[hd]][hd % 2])
              for hd in heads]
    logits = [jnp.where(in_cur, scores[hd][:, ATTN_BLOCK:], scores[hd][:, :ATTN_BLOCK])
              + bias_ref[table, hd] for hd in heads]
    mx = [jnp.maximum(jnp.max(logits[hd], axis=-1, keepdims=True), sink_ref[hd]) for hd in heads]
    pr = [jnp.exp(logits[hd] - mx[hd]) for hd in heads]
    inv = [1.0 / (jnp.sum(pr[hd], axis=-1, keepdims=True) + jnp.exp(sink_ref[hd] - mx[hd]))
           for hd in heads]
    pv = [_dot(jnp.concatenate([jnp.where(in_cur, 0.0, pr[hd]), jnp.where(in_cur, pr[hd], 0.0)],
                               axis=1).astype(BF16), v_ab[kv_of[hd]][hd % 2]) * inv[hd]
          for hd in heads]
    for p in range(N_ATTN_HEADS // 2):
        o_ref[:, p * LANES:(p + 1) * LANES] = (pv[2 * p] + pv[2 * p + 1]).astype(o_ref.dtype)


def _swa_attention(proj, rel_bias, sinks, batch, seq):
    nb = seq // ATTN_BLOCK
    bucket = jnp.asarray(_t5_bucket_table())
    q_blk = COL_Q // ATTN_Q
    k_blk = COL_K // ATTN_KV
    v_blk = COL_V // ATTN_KV

    def cur(col):
        return lambda b, n: (b * nb + n, col)

    def prev(col):
        return lambda b, n: (b * nb + jnp.maximum(n - 1, 0), col)

    smem = pl.BlockSpec(memory_space=pltpu.SMEM)
    return pl.pallas_call(
        _attn_kernel,
        out_shape=jax.ShapeDtypeStruct((batch * seq, ATTN_Q), BF16),
        grid=(batch, nb),
        in_specs=[
            pl.BlockSpec((ATTN_BLOCK, ATTN_BLOCK), lambda b, n: (0, 0)),
            smem, smem,
            pl.BlockSpec((ATTN_BLOCK, ATTN_Q), cur(q_blk)),
            pl.BlockSpec((ATTN_BLOCK, ATTN_KV), prev(k_blk)),
            pl.BlockSpec((ATTN_BLOCK, ATTN_KV), cur(k_blk)),
            pl.BlockSpec((ATTN_BLOCK, ATTN_KV), prev(v_blk)),
            pl.BlockSpec((ATTN_BLOCK, ATTN_KV), cur(v_blk)),
        ],
        out_specs=pl.BlockSpec((ATTN_BLOCK, ATTN_Q), lambda b, n: (b * nb + n, 0)),
        scratch_shapes=[pltpu.VMEM((2, N_ATTN_HEADS, ATTN_BLOCK, ATTN_BLOCK), F32)],
        compiler_params=pltpu.CompilerParams(
            dimension_semantics=("arbitrary", "arbitrary"),
            vmem_limit_bytes=VMEM_LIMIT_BYTES),
        name="swa_attn",
    )(bucket, rel_bias, sinks, proj, proj, proj, proj, proj)


GDN_CHUNKS = 4
GDN_TOK = GDN_CHUNKS * CHUNK


def _gdn_kernel(xprev_ref, x_ref, z_ref, ab_ref, convw_ref, alog_ref, dtb_ref, normw_ref,
                o_ref, state_ref):
    c = pl.program_id(1)

    @pl.when(c == 0)
    def _():
        state_ref[...] = jnp.zeros(state_ref.shape, F32)

    prev = jnp.where(c == 0, 0.0, xprev_ref[...].astype(F32))
    xe = jnp.concatenate([prev[SUBLANES:], x_ref[...].astype(F32)], axis=0)

    def shift_rows(t, s):
        tiles = t.reshape(t.shape[0] // SUBLANES, SUBLANES, t.shape[1])
        rot = pltpu.roll(tiles, s, axis=1)
        sub = lax.broadcasted_iota(jnp.int32, tiles.shape, 1)
        return jnp.where(sub < s, pltpu.roll(rot, 1, axis=0), rot).reshape(t.shape)

    w = [convw_ref[j:j + 1, :] for j in range(CONV_WIDTH)]
    x2 = shift_rows(xe, 2)
    y = w[3] * xe + w[1] * x2 + shift_rows(w[2] * xe + w[0] * x2, 1)
    y = y[SUBLANES:, :]
    y = y * jax.nn.sigmoid(y)

    ab = ab_ref[...]
    g_all = -jnp.exp(alog_ref[...]) * jax.nn.softplus(ab + dtb_ref[...])
    beta_all = jax.nn.sigmoid(ab)
    row = lax.broadcasted_iota(jnp.int32, (CHUNK, CHUNK), 0)
    colm = lax.broadcasted_iota(jnp.int32, (CHUNK, CHUNK), 1)
    tril = row >= colm
    strict = row > colm
    ltri = tril.astype(F32)
    eye = (row == colm).astype(F32)
    normw = normw_ref[...]
    d = DELTA_HEAD_DIM

    items = [(ci, h) for ci in range(GDN_CHUNKS) for h in range(N_DELTA_HEADS)]
    idx = range(len(items))

    def rows(ci):
        return slice(ci * CHUNK, (ci + 1) * CHUNK)

    g_cum = [jnp.dot(ltri, g_all[rows(ci)], precision=lax.Precision.HIGHEST,
                     preferred_element_type=F32) for ci in range(GDN_CHUNKS)]
    g_cum_t = [lax.dot_general(g_all[rows(ci)], ltri, (((0,), (1,)), ((), ())),
                               precision=lax.Precision.HIGHEST, preferred_element_type=F32)
               for ci in range(GDN_CHUNKS)]

    def l2n(t):
        return t * lax.rsqrt(jnp.sum(t * t, axis=-1, keepdims=True) + RMS_EPS)

    q = [l2n(y[rows(ci), h * d:(h + 1) * d]) * (d ** -0.5) for ci, h in items]
    k = [l2n(y[rows(ci), DELTA_WIDTH + h * d:DELTA_WIDTH + (h + 1) * d]) for ci, h in items]
    v = [y[rows(ci), 2 * DELTA_WIDTH + h * d:2 * DELTA_WIDTH + (h + 1) * d] for ci, h in items]
    beta = [beta_all[rows(ci), N_DELTA_HEADS + h:N_DELTA_HEADS + h + 1] for ci, h in items]
    gc = [g_cum[ci][:, h:h + 1] for ci, h in items]
    g_last = [gc[i][CHUNK - 1:CHUNK, :] for i in idx]
    exp_g = [jnp.exp(gc[i]) for i in idx]
    decay = [jnp.exp(jnp.where(tril, gc[i] - g_cum_t[ci][h:h + 1, :], -jnp.inf))
             for i, (ci, h) in enumerate(items)]

    kb = [k[i] * beta[i] for i in idx]
    k_bf = [k[i].astype(BF16) for i in idx]
    kq = [_dot_nt(jnp.concatenate([kb[i], q[i]], axis=0).astype(BF16), k_bf[i]) for i in idx]
    attn = [(kq[i][CHUNK:2 * CHUNK] * decay[i]).astype(BF16) for i in idx]

    nj = [jnp.where(strict, -(kq[i][0:CHUNK] * decay[i]), 0.0) for i in idx]
    pj = [eye + nj[i] for i in idx]
    nj = [_dot(nj[i].astype(BF16), nj[i].astype(BF16)) for i in idx]
    for _ in range(4):
        both = [_dot(jnp.concatenate([pj[i], nj[i]], axis=0).astype(BF16), nj[i].astype(BF16))
                for i in idx]
        pj = [pj[i] + both[i][0:CHUNK] for i in idx]
        nj = [both[i][CHUNK:2 * CHUNK] for i in idx]
    pj = [pj[i] + _dot(pj[i].astype(BF16), nj[i].astype(BF16)) for i in idx]

    sol = [_dot(pj[i].astype(BF16),
                jnp.concatenate([v[i] * beta[i], kb[i] * exp_g[i]], axis=1).astype(BF16))
           for i in idx]
    wq = [jnp.concatenate([sol[i][:, d:2 * d], q[i] * exp_g[i]], axis=0).astype(BF16) for i in idx]
    k_dec = [(k[i] * jnp.exp(g_last[i] - gc[i])).astype(BF16) for i in idx]

    state = [state_ref[h] for h in range(N_DELTA_HEADS)]
    heads = range(N_DELTA_HEADS)
    o = []
    for ci in range(GDN_CHUNKS):
        it = [ci * N_DELTA_HEADS + h for h in heads]
        ws = [_dot(wq[i], state[h].astype(BF16)) for h, i in enumerate(it)]
        v_new = [(sol[i][:, 0:d] - ws[h][0:CHUNK]).astype(BF16) for h, i in enumerate(it)]
        o += [ws[h][CHUNK:2 * CHUNK] + _dot(attn[i], v_new[h]) for h, i in enumerate(it)]
        state = [state[h] * jnp.exp(g_last[i]) + _dot_tn(k_dec[i], v_new[h]) for h, i in enumerate(it)]
    for h in heads:
        state_ref[h] = state[h]

    for i, (ci, h) in enumerate(items):
        oh = o[i] * lax.rsqrt(jnp.mean(o[i] * o[i], axis=-1, keepdims=True) + RMS_EPS) * normw
        zh = z_ref[rows(ci), h * d:(h + 1) * d].astype(F32)
        o_ref[rows(ci), h * d:(h + 1) * d] = (oh * (zh * jax.nn.sigmoid(zh))).astype(o_ref.dtype)


def _gdn(proj, ab, conv_w, alog_row, dtb_row, normw_row, batch, seq):
    ns = seq // GDN_TOK
    prev_per_step = GDN_TOK // PREV_ROWS

    def cur(col):
        return lambda b, c: (b * ns + c, col)

    def prev_rows(b, c):
        return (jnp.maximum((b * ns + c) * prev_per_step - 1, 0), COL_DQKV // DELTA_QKV)

    const = lambda b, c: (0, 0)
    return pl.pallas_call(
        _gdn_kernel,
        out_shape=jax.ShapeDtypeStruct((batch * seq, DELTA_WIDTH), BF16),
        grid=(batch, ns),
        in_specs=[
            pl.BlockSpec((PREV_ROWS, DELTA_QKV), prev_rows),
            pl.BlockSpec((GDN_TOK, DELTA_QKV), cur(COL_DQKV // DELTA_QKV)),
            pl.BlockSpec((GDN_TOK, DELTA_WIDTH), cur(COL_Z // DELTA_WIDTH)),
            pl.BlockSpec((GDN_TOK, LANES), cur(0)),
            pl.BlockSpec((CONV_WIDTH, DELTA_QKV), const),
            pl.BlockSpec((1, LANES), const),
            pl.BlockSpec((1, LANES), const),
            pl.BlockSpec((1, DELTA_HEAD_DIM), const),
        ],
        out_specs=pl.BlockSpec((GDN_TOK, DELTA_WIDTH), lambda b, c: (b * ns + c, 0)),
        scratch_shapes=[
            pltpu.VMEM((N_DELTA_HEADS, DELTA_HEAD_DIM, DELTA_HEAD_DIM), F32),
        ],
        compiler_params=pltpu.CompilerParams(
            dimension_semantics=("arbitrary", "arbitrary"),
            vmem_limit_bytes=VMEM_LIMIT_BYTES),
        name="gdn",
    )(proj, proj, proj, ab, conv_w, alog_row, dtb_row, normw_row)


OUT_TM = 512


def _out_ln1_kernel(x_ref, attn_ref, delta_ref, wo_ref, g_ref, b_ref, o_ref):
    mixed = (_dot(attn_ref[...], wo_ref[0:ATTN_Q, :])
             + _dot(delta_ref[...], wo_ref[ATTN_Q:ATTN_Q + DELTA_WIDTH, :]))
    y = DN_ALPHA * x_ref[...] + mixed
    o_ref[...] = _layer_norm(y, g_ref[...], b_ref[...])


def _out_ln1(x2, attn_out, delta_out, w_o, g, b):
    t = x2.shape[0]
    const = lambda i: (0, 0)
    return pl.pallas_call(
        _out_ln1_kernel,
        out_shape=jax.ShapeDtypeStruct((t, D_MODEL), F32),
        grid=(t // OUT_TM,),
        in_specs=[
            pl.BlockSpec((OUT_TM, D_MODEL), lambda i: (i, 0)),
            pl.BlockSpec((OUT_TM, ATTN_Q), lambda i: (i, 0)),
            pl.BlockSpec((OUT_TM, DELTA_WIDTH), lambda i: (i, 0)),
            pl.BlockSpec((ATTN_Q + DELTA_WIDTH, D_MODEL), const),
            pl.BlockSpec((1, D_MODEL), const),
            pl.BlockSpec((1, D_MODEL), const),
        ],
        out_specs=pl.BlockSpec((OUT_TM, D_MODEL), lambda i: (i, 0)),
        compiler_params=pltpu.CompilerParams(
            dimension_semantics=("arbitrary",),
            vmem_limit_bytes=VMEM_LIMIT_BYTES),
        name="out_ln1",
    )(x2, attn_out, delta_out, w_o, g, b)


MLP_TM = 1024
MLP_TF = 512


def _mlp_ln2_kernel(x_ref, wup_ref, wdn_ref, g_ref, b_ref, o_ref, xb_ref):
    j = pl.program_id(1)

    @pl.when(j == 0)
    def _():
        x = x_ref[...]
        xb_ref[...] = x.astype(BF16)
        o_ref[...] = DN_ALPHA * x

    a = jnp.maximum(_dot(xb_ref[...], wup_ref[...]), 0.0)
    o_ref[...] += _dot((a * a).astype(BF16), wdn_ref[...])

    @pl.when(j == pl.num_programs(1) - 1)
    def _():
        o_ref[...] = _layer_norm(o_ref[...], g_ref[...], b_ref[...])


def _mlp_ln2(x1, w_up, w_down, g, b):
    t = x1.shape[0]
    const = lambda i, j: (0, 0)
    return pl.pallas_call(
        _mlp_ln2_kernel,
        out_shape=jax.ShapeDtypeStruct((t, D_MODEL), F32),
        grid=(t // MLP_TM, D_FF // MLP_TF),
        in_specs=[
            pl.BlockSpec((MLP_TM, D_MODEL), lambda i, j: (i, 0)),
            pl.BlockSpec((D_MODEL, MLP_TF), lambda i, j: (0, j)),
            pl.BlockSpec((MLP_TF, D_MODEL), lambda i, j: (j, 0)),
            pl.BlockSpec((1, D_MODEL), const),
            pl.BlockSpec((1, D_MODEL), const),
        ],
        out_specs=pl.BlockSpec((MLP_TM, D_MODEL), lambda i, j: (i, 0)),
        scratch_shapes=[pltpu.VMEM((MLP_TM, D_MODEL), BF16)],
        compiler_params=pltpu.CompilerParams(
            dimension_semantics=("arbitrary", "arbitrary"),
            vmem_limit_bytes=VMEM_LIMIT_BYTES),
        name="mlp_ln2",
    )(x1, w_up, w_down, g, b)


def _split_w_in(w_in):
    w_main = w_in[:, :SRC_AB].astype(BF16)
    w_z = w_in[:, SRC_Z:].astype(BF16)
    w_ab = jnp.pad(w_in[:, SRC_AB:SRC_Z], ((0, 0), (0, MXU_COLS - 2 * N_DELTA_HEADS))).astype(BF16)
    return w_main, w_z, w_ab


def _lane_row(v):
    return jnp.zeros((1, LANES), F32).at[0, :v.shape[0]].set(v.astype(F32))


def kernel(x, w_in, conv_w, a_log, dt_bias, delta_norm_w, attn_sinks, rel_bias, w_o, ln1_g, ln1_b,
           w_up, w_down, ln2_g, ln2_b):
    batch, seq, d = x.shape
    assert d == D_MODEL and seq % ATTN_BLOCK == 0 and (batch * seq) % IN_TM == 0
    assert w_in.shape[0] == DEPTH
    x2 = x.reshape(batch * seq, d)
    for l in range(DEPTH):
        proj, ab = _in_proj(x2, *_split_w_in(w_in[l]))
        attn_out = _swa_attention(proj, rel_bias.astype(F32), attn_sinks[l].astype(F32), batch, seq)
        delta_out = _gdn(proj, ab, conv_w[l].reshape(CONV_WIDTH, DELTA_QKV).astype(F32),
                         _lane_row(a_log[l]), _lane_row(dt_bias[l]),
                         delta_norm_w[l].reshape(1, DELTA_HEAD_DIM).astype(F32), batch, seq)
        x1 = _out_ln1(x2, attn_out, delta_out, w_o[l].astype(BF16),
                      ln1_g[l].reshape(1, d), ln1_b[l].reshape(1, d))
        x2 = _mlp_ln2(x1, w_up[l].astype(BF16), w_down[l].astype(BF16),
                      ln2_g[l].reshape(1, d), ln2_b[l].reshape(1, d))
    return x2.reshape(batch, seq, d)
```

```python
import functools
import math

import jax
import jax.numpy as jnp
import numpy as np
from jax import lax
from jax.experimental import pallas as pl
from jax.experimental.pallas import tpu as pltpu

F32 = jnp.float32
BF16 = jnp.bfloat16

D_MODEL = 2048
ATTN_HEAD_DIM = 64
N_ATTN_HEADS = 16
N_KV_HEADS = 4
ATTN_BLOCK = 128
WINDOW = 128
NEG_INF = -1e30
N_BUCKETS = 32
MAX_DISTANCE = 128
DELTA_HEAD_DIM = 128
N_DELTA_HEADS = 8
DELTA_WIDTH = N_DELTA_HEADS * DELTA_HEAD_DIM
CONV_WIDTH = 4
CHUNK = 64
D_FF = 4 * D_MODEL
DEPTH = 1
DN_ALPHA = (2.0 * DEPTH) ** 0.25
LN_EPS = 1e-5
RMS_EPS = 1e-6

ATTN_Q = N_ATTN_HEADS * ATTN_HEAD_DIM
ATTN_KV = N_KV_HEADS * ATTN_HEAD_DIM
DELTA_QKV = 3 * DELTA_WIDTH

LANES = 128
SUBLANES = 8
PREV_ROWS = 2 * SUBLANES
VMEM_LIMIT_BYTES = 56 * 1024 * 1024

COL_DQKV = 0
COL_Z = COL_DQKV + DELTA_QKV
COL_Q = COL_Z + DELTA_WIDTH
COL_K = COL_Q + ATTN_Q
COL_V = COL_K + ATTN_KV
PROJ_COLS = COL_V + ATTN_KV

SRC_Q = 0
SRC_DQKV = ATTN_Q + 2 * ATTN_KV
SRC_AB = SRC_DQKV + DELTA_QKV
SRC_Z = SRC_AB + 2 * N_DELTA_HEADS
MXU_COLS = 256


def _dot(a, b):
    return jnp.dot(a, b, preferred_element_type=F32)


def _dot_nt(a, b):
    return lax.dot_general(a, b, (((1,), (1,)), ((), ())), preferred_element_type=F32)


def _dot_tn(a, b):
    return lax.dot_general(a, b, (((0,), (0,)), ((), ())), preferred_element_type=F32)


def _layer_norm(y, g, b):
    mu = jnp.mean(y, axis=-1, keepdims=True)
    yc = y - mu
    var = jnp.mean(yc * yc, axis=-1, keepdims=True)
    return yc * lax.rsqrt(var + LN_EPS) * g + b


IN_TM = 512
IN_CHUNK = 512


def _in_proj_kernel(x_ref, wmain_ref, wz_ref, wab_ref, o_ref, ab_ref):
    xb = x_ref[...].astype(BF16)
    pieces = ((wmain_ref, SRC_DQKV, DELTA_QKV, COL_DQKV),
              (wz_ref, 0, DELTA_WIDTH, COL_Z),
              (wmain_ref, SRC_Q, ATTN_Q + 2 * ATTN_KV, COL_Q))
    for w_ref, src, width, dst in pieces:
        for c in range(0, width, IN_CHUNK):
            o_ref[:, dst + c:dst + c + IN_CHUNK] = _dot(
                xb, w_ref[:, src + c:src + c + IN_CHUNK]).astype(o_ref.dtype)
    ab_ref[...] = _dot(xb, wab_ref[...])[:, 0:LANES]


def _in_proj(x2, w_main, w_z, w_ab):
    t = x2.shape[0]
    const = lambda i: (0, 0)
    resident = pl.Buffered(1)
    return pl.pallas_call(
        _in_proj_kernel,
        out_shape=(jax.ShapeDtypeStruct((t, PROJ_COLS), BF16),
                   jax.ShapeDtypeStruct((t, LANES), F32)),
        grid=(t // IN_TM,),
        in_specs=[
            pl.BlockSpec((IN_TM, D_MODEL), lambda i: (i, 0)),
            pl.BlockSpec(w_main.shape, const, pipeline_mode=resident),
            pl.BlockSpec(w_z.shape, const, pipeline_mode=resident),
            pl.BlockSpec(w_ab.shape, const, pipeline_mode=resident),
        ],
        out_specs=(pl.BlockSpec((IN_TM, PROJ_COLS), lambda i: (i, 0)),
                   pl.BlockSpec((IN_TM, LANES), lambda i: (i, 0))),
        compiler_params=pltpu.CompilerParams(
            dimension_semantics=("arbitrary",),
            vmem_limit_bytes=VMEM_LIMIT_BYTES),
        name="in_proj",
    )(x2, w_main, w_z, w_ab)


def _t5_bucket_table():
    qi = np.arange(ATTN_BLOCK, dtype=np.int64)[:, None]
    kj = np.arange(ATTN_BLOCK, dtype=np.int64)[None, :]
    dist = np.where(kj <= qi, qi - kj, qi + ATTN_BLOCK - kj)
    assert dist.min() >= 0 and dist.max() < WINDOW
    max_exact = N_BUCKETS // 2
    nf = np.maximum(dist, 1).astype(np.float64)
    large = max_exact + (np.log(nf / max_exact) / math.log(MAX_DISTANCE / max_exact)
                         * (N_BUCKETS - max_exact)).astype(np.int64)
    large = np.minimum(large, N_BUCKETS - 1)
    return np.where(dist < max_exact, dist, large).astype(np.int32)


def _attn_kernel(bucket_ref, relb_ref, sink_ref, q_ref, kp_ref, kc_ref, vp_ref, vc_ref,
                 o_ref, bias_ref):
    b = pl.program_id(0)
    n = pl.program_id(1)
    row = lax.broadcasted_iota(jnp.int32, (ATTN_BLOCK, ATTN_BLOCK), 0)
    col = lax.broadcasted_iota(jnp.int32, (ATTN_BLOCK, ATTN_BLOCK), 1)
    in_cur = col <= row

    @pl.when((b == 0) & (n == 0))
    def _():
        bucket = bucket_ref[...]

        def head_body(h, carry):
            acc = jnp.zeros(bucket.shape, F32)
            for bk in range(N_BUCKETS):
                acc = jnp.where(bucket == bk, relb_ref[bk, h], acc)
            bias_ref[0, h] = acc
            bias_ref[1, h] = jnp.where(in_cur, acc, NEG_INF)
            return carry

        lax.fori_loop(0, N_ATTN_HEADS, head_body, 0)

    table = jnp.where(n == 0, 1, 0)
    lane = lax.broadcasted_iota(jnp.int32, (2 * ATTN_BLOCK, LANES), 1)
    lo = lane < ATTN_HEAD_DIM

    q = q_ref[...] * (ATTN_HEAD_DIM ** -0.5)
    kcat = jnp.concatenate([kp_ref[...], kc_ref[...]], axis=0).astype(F32)
    vcat = jnp.concatenate([vp_ref[...], vc_ref[...]], axis=0).astype(F32)

    k_ab, v_ab = [], []
    for m in range(N_KV_HEADS // 2):
        kg = kcat[:, m * LANES:(m + 1) * LANES]
        vg = vcat[:, m * LANES:(m + 1) * LANES]
        kg_sw = pltpu.roll(kg, ATTN_HEAD_DIM, axis=1)
        vg_sw = pltpu.roll(vg, ATTN_HEAD_DIM, axis=1)
        for t in range(2):
            k_src, k_alt = (kg, kg_sw) if t == 0 else (kg_sw, kg)
            v_src, v_alt = (vg, vg_sw) if t == 0 else (vg_sw, vg)
            k_ab.append((jnp.where(lo, k_src, 0.0).astype(BF16), jnp.where(lo, 0.0, k_alt).astype(BF16)))
            v_ab.append((jnp.where(lo, v_src, 0.0).astype(BF16), jnp.where(lo, 0.0, v_alt).astype(BF16)))

    heads = range(N_ATTN_HEADS)
    kv_of = [hd // (N_ATTN_HEADS // N_KV_HEADS) for hd in heads]
    scores = [_dot_nt(q[:, (hd // 2) * LANES:(hd // 2 + 1) * LANES], k_ab[kv_of[hd]][hd % 2])
              for hd in heads]
    logits = [jnp.where(in_cur, scores[hd][:, ATTN_BLOCK:], scores[hd][:, :ATTN_BLOCK])
              + bias_ref[table, hd] for hd in heads]
    mx = [jnp.maximum(jnp.max(logits[hd], axis=-1, keepdims=True), sink_ref[hd]) for hd in heads]
    pr = [jnp.exp(logits[hd] - mx[hd]) for hd in heads]
    inv = [1.0 / (jnp.sum(pr[hd], axis=-1, keepdims=True) + jnp.exp(sink_ref[hd] - mx[hd]))
           for hd in heads]
    pv = [_dot(jnp.concatenate([jnp.where(in_cur, 0.0, pr[hd]), jnp.where(in_cur, pr[hd], 0.0)],
                               axis=1).astype(BF16), v_ab[kv_of[hd]][hd % 2]) * inv[hd]
          for hd in heads]
    for p in range(N_ATTN_HEADS // 2):
        o_ref[:, p * LANES:(p + 1) * LANES] = (pv[2 * p] + pv[2 * p + 1]).astype(o_ref.dtype)


def _swa_attention(proj, rel_bias, sinks, batch, seq):
    nb = seq // ATTN_BLOCK
    bucket = jnp.asarray(_t5_bucket_table())
    q_blk = COL_Q // ATTN_Q
    k_blk = COL_K // ATTN_KV
    v_blk = COL_V // ATTN_KV

    def cur(col):
        return lambda b, n: (b * nb + n, col)

    def prev(col):
        return lambda b, n: (b * nb + jnp.maximum(n - 1, 0), col)

    smem = pl.BlockSpec(memory_space=pltpu.SMEM)
    return pl.pallas_call(
        _attn_kernel,
        out_shape=jax.ShapeDtypeStruct((batch * seq, ATTN_Q), BF16),
        grid=(batch, nb),
        in_specs=[
            pl.BlockSpec((ATTN_BLOCK, ATTN_BLOCK), lambda b, n: (0, 0)),
            smem, smem,
            pl.BlockSpec((ATTN_BLOCK, ATTN_Q), cur(q_blk)),
            pl.BlockSpec((ATTN_BLOCK, ATTN_KV), prev(k_blk)),
            pl.BlockSpec((ATTN_BLOCK, ATTN_KV), cur(k_blk)),
            pl.BlockSpec((ATTN_BLOCK, ATTN_KV), prev(v_blk)),
            pl.BlockSpec((ATTN_BLOCK, ATTN_KV), cur(v_blk)),
        ],
        out_specs=pl.BlockSpec((ATTN_BLOCK, ATTN_Q), lambda b, n: (b * nb + n, 0)),
        scratch_shapes=[pltpu.VMEM((2, N_ATTN_HEADS, ATTN_BLOCK, ATTN_BLOCK), F32)],
        compiler_params=pltpu.CompilerParams(
            dimension_semantics=("arbitrary", "arbitrary"),
            vmem_limit_bytes=VMEM_LIMIT_BYTES),
        name="swa_attn",
    )(bucket, rel_bias, sinks, proj, proj, proj, proj, proj)


GDN_CHUNKS = 4
GDN_TOK = GDN_CHUNKS * CHUNK


def _gdn_kernel(xprev_ref, x_ref, z_ref, ab_ref, convw_ref, alog_ref, dtb_ref, normw_ref,
                o_ref, state_ref):
    c = pl.program_id(1)

    @pl.when(c == 0)
    def _():
        state_ref[...] = jnp.zeros(state_ref.shape, F32)

    prev = jnp.where(c == 0, 0.0, xprev_ref[...].astype(F32))
    xe = jnp.concatenate([prev[SUBLANES:], x_ref[...].astype(F32)], axis=0)

    def shift_rows(t, s):
        tiles = t.reshape(t.shape[0] // SUBLANES, SUBLANES, t.shape[1])
        rot = pltpu.roll(tiles, s, axis=1)
        sub = lax.broadcasted_iota(jnp.int32, tiles.shape, 1)
        return jnp.where(sub < s, pltpu.roll(rot, 1, axis=0), rot).reshape(t.shape)

    w = [convw_ref[j:j + 1, :] for j in range(CONV_WIDTH)]
    x2 = shift_rows(xe, 2)
    y = w[3] * xe + w[1] * x2 + shift_rows(w[2] * xe + w[0] * x2, 1)
    y = y[SUBLANES:, :]
    y = y * jax.nn.sigmoid(y)

    ab = ab_ref[...]
    g_all = -jnp.exp(alog_ref[...]) * jax.nn.softplus(ab + dtb_ref[...])
    beta_all = jax.nn.sigmoid(ab)
    row = lax.broadcasted_iota(jnp.int32, (CHUNK, CHUNK), 0)
    colm = lax.broadcasted_iota(jnp.int32, (CHUNK, CHUNK), 1)
    tril = row >= colm
    strict = row > colm
    ltri = tril.astype(F32)
    eye = (row == colm).astype(F32)
    normw = normw_ref[...]
    d = DELTA_HEAD_DIM

    items = [(ci, h) for ci in range(GDN_CHUNKS) for h in range(N_DELTA_HEADS)]
    idx = range(len(items))

    def rows(ci):
        return slice(ci * CHUNK, (ci + 1) * CHUNK)

    g_cum = [jnp.dot(ltri, g_all[rows(ci)], precision=lax.Precision.HIGHEST,
                     preferred_element_type=F32) for ci in range(GDN_CHUNKS)]
    g_cum_t = [lax.dot_general(g_all[rows(ci)], ltri, (((0,), (1,)), ((), ())),
                               precision=lax.Precision.HIGHEST, preferred_element_type=F32)
               for ci in range(GDN_CHUNKS)]

    def l2n(t):
        return t * lax.rsqrt(jnp.sum(t * t, axis=-1, keepdims=True) + RMS_EPS)

    q = [l2n(y[rows(ci), h * d:(h + 1) * d]) * (d ** -0.5) for ci, h in items]
    k = [l2n(y[rows(ci), DELTA_WIDTH + h * d:DELTA_WIDTH + (h + 1) * d]) for ci, h in items]
    v = [y[rows(ci), 2 * DELTA_WIDTH + h * d:2 * DELTA_WIDTH + (h + 1) * d] for ci, h in items]
    beta = [beta_all[rows(ci), N_DELTA_HEADS + h:N_DELTA_HEADS + h + 1] for ci, h in items]
    gc = [g_cum[ci][:, h:h + 1] for ci, h in items]
    g_last = [gc[i][CHUNK - 1:CHUNK, :] for i in idx]
    exp_g = [jnp.exp(gc[i]) for i in idx]
    decay = [jnp.exp(jnp.where(tril, gc[i] - g_cum_t[ci][h:h + 1, :], -jnp.inf))
             for i, (ci, h) in enumerate(items)]

    kb = [k[i] * beta[i] for i in idx]
    k_bf = [k[i].astype(BF16) for i in idx]
    kq = [_dot_nt(jnp.concatenate([kb[i], q[i]], axis=0).astype(BF16), k_bf[i]) for i in idx]
    attn = [(kq[i][CHUNK:2 * CHUNK] * decay[i]).astype(BF16) for i in idx]

    nj = [jnp.where(strict, -(kq[i][0:CHUNK] * decay[i]), 0.0) for i in idx]
    pj = [eye + nj[i] for i in idx]
    nj = [_dot(nj[i].astype(BF16), nj[i].astype(BF16)) for i in idx]
    for _ in range(4):
        both = [_dot(jnp.concatenate([pj[i], nj[i]], axis=0).astype(BF16), nj[i].astype(BF16))
                for i in idx]
        pj = [pj[i] + both[i][0:CHUNK] for i in idx]
        nj = [both[i][CHUNK:2 * CHUNK] for i in idx]
    pj = [pj[i] + _dot(pj[i].astype(BF16), nj[i].astype(BF16)) for i in idx]

    sol = [_dot(pj[i].astype(BF16),
                jnp.concatenate([v[i] * beta[i], kb[i] * exp_g[i]], axis=1).astype(BF16))
           for i in idx]
    wq = [jnp.concatenate([sol[i][:, d:2 * d], q[i] * exp_g[i]], axis=0).astype(BF16) for i in idx]
    k_dec = [(k[i] * jnp.exp(g_last[i] - gc[i])).astype(BF16) for i in idx]

    state = [state_ref[h] for h in range(N_DELTA_HEADS)]
    heads = range(N_DELTA_HEADS)
    o = []
    for ci in range(GDN_CHUNKS):
        it = [ci * N_DELTA_HEADS + h for h in heads]
        ws = [_dot(wq[i], state[h].astype(BF16)) for h, i in enumerate(it)]
        v_new = [(sol[i][:, 0:d] - ws[h][0:CHUNK]).astype(BF16) for h, i in enumerate(it)]
        o += [ws[h][CHUNK:2 * CHUNK] + _dot(attn[i], v_new[h]) for h, i in enumerate(it)]
        state = [state[h] * jnp.exp(g_last[i]) + _dot_tn(k_dec[i], v_new[h]) for h, i in enumerate(it)]
    for h in heads:
        state_ref[h] = state[h]

    for i, (ci, h) in enumerate(items):
        oh = o[i] * lax.rsqrt(jnp.mean(o[i] * o[i], axis=-1, keepdims=True) + RMS_EPS) * normw
        zh = z_ref[rows(ci), h * d:(h + 1) * d].astype(F32)
        o_ref[rows(ci), h * d:(h + 1) * d] = (oh * (zh * jax.nn.sigmoid(zh))).astype(o_ref.dtype)


def _gdn(proj, ab, conv_w, alog_row, dtb_row, normw_row, batch, seq):
    ns = seq // GDN_TOK
    prev_per_step = GDN_TOK // PREV_ROWS

    def cur(col):
        return lambda b, c: (b * ns + c, col)

    def prev_rows(b, c):
        return (jnp.maximum((b * ns + c) * prev_per_step - 1, 0), COL_DQKV // DELTA_QKV)

    const = lambda b, c: (0, 0)
    return pl.pallas_call(
        _gdn_kernel,
        out_shape=jax.ShapeDtypeStruct((batch * seq, DELTA_WIDTH), BF16),
        grid=(batch, ns),
        in_specs=[
            pl.BlockSpec((PREV_ROWS, DELTA_QKV), prev_rows),
            pl.BlockSpec((GDN_TOK, DELTA_QKV), cur(COL_DQKV // DELTA_QKV)),
            pl.BlockSpec((GDN_TOK, DELTA_WIDTH), cur(COL_Z // DELTA_WIDTH)),
            pl.BlockSpec((GDN_TOK, LANES), cur(0)),
            pl.BlockSpec((CONV_WIDTH, DELTA_QKV), const),
            pl.BlockSpec((1, LANES), const),
            pl.BlockSpec((1, LANES), const),
            pl.BlockSpec((1, DELTA_HEAD_DIM), const),
        ],
        out_specs=pl.BlockSpec((GDN_TOK, DELTA_WIDTH), lambda b, c: (b * ns + c, 0)),
        scratch_shapes=[
            pltpu.VMEM((N_DELTA_HEADS, DELTA_HEAD_DIM, DELTA_HEAD_DIM), F32),
        ],
        compiler_params=pltpu.CompilerParams(
            dimension_semantics=("arbitrary", "arbitrary"),
            vmem_limit_bytes=VMEM_LIMIT_BYTES),
        name="gdn",
    )(proj, proj, proj, ab, conv_w, alog_row, dtb_row, normw_row)


OUT_TM = 512


def _out_ln1_kernel(x_ref, attn_ref, delta_ref, wo_ref, g_ref, b_ref, o_ref, ob_ref):
    mixed = (_dot(attn_ref[...], wo_ref[0:ATTN_Q, :])
             + _dot(delta_ref[...], wo_ref[ATTN_Q:ATTN_Q + DELTA_WIDTH, :]))
    y = _layer_norm(DN_ALPHA * x_ref[...] + mixed, g_ref[...], b_ref[...])
    o_ref[...] = y
    ob_ref[...] = y.astype(BF16)


def _out_ln1(x2, attn_out, delta_out, w_o, g, b):
    t = x2.shape[0]
    const = lambda i: (0, 0)
    tile = lambda i: (i, 0)
    return pl.pallas_call(
        _out_ln1_kernel,
        out_shape=(jax.ShapeDtypeStruct((t, D_MODEL), F32),
                   jax.ShapeDtypeStruct((t, D_MODEL), BF16)),
        grid=(t // OUT_TM,),
        in_specs=[
            pl.BlockSpec((OUT_TM, D_MODEL), tile),
            pl.BlockSpec((OUT_TM, ATTN_Q), tile),
            pl.BlockSpec((OUT_TM, DELTA_WIDTH), tile),
            pl.BlockSpec((ATTN_Q + DELTA_WIDTH, D_MODEL), const),
            pl.BlockSpec((1, D_MODEL), const),
            pl.BlockSpec((1, D_MODEL), const),
        ],
        out_specs=(pl.BlockSpec((OUT_TM, D_MODEL), tile),
                   pl.BlockSpec((OUT_TM, D_MODEL), tile)),
        compiler_params=pltpu.CompilerParams(
            dimension_semantics=("arbitrary",),
            vmem_limit_bytes=VMEM_LIMIT_BYTES),
        name="out_ln1",
    )(x2, attn_out, delta_out, w_o, g, b)


MLP_TM = 1024
MLP_TF = 1024
MLP_STEPS = D_FF // MLP_TF
MLP_RES_ROWS = MLP_TM // MLP_STEPS


def _mlp_ln2_kernel(xb_ref, xres_ref, wup_ref, wdn_ref, g_ref, b_ref, o_ref):
    j = pl.program_id(1)

    def accumulate(first):
        a = jnp.maximum(_dot(xb_ref[...], wup_ref[...]), 0.0)
        upd = _dot((a * a).astype(BF16), wdn_ref[...])
        if first:
            o_ref[...] = upd
        else:
            o_ref[...] += upd

    pl.when(j == 0)(functools.partial(accumulate, True))
    pl.when(j > 0)(functools.partial(accumulate, False))

    rows = pl.ds(pl.multiple_of(j * MLP_RES_ROWS, MLP_RES_ROWS), MLP_RES_ROWS)
    o_ref[rows, :] += DN_ALPHA * xres_ref[...]

    @pl.when(j == MLP_STEPS - 1)
    def _():
        o_ref[...] = _layer_norm(o_ref[...], g_ref[...], b_ref[...])


def _mlp_ln2(x1, x1_bf, w_up, w_down, g, b):
    t = x1.shape[0]
    const = lambda i, j: (0, 0)
    return pl.pallas_call(
        _mlp_ln2_kernel,
        out_shape=jax.ShapeDtypeStruct((t, D_MODEL), F32),
        grid=(t // MLP_TM, MLP_STEPS),
        in_specs=[
            pl.BlockSpec((MLP_TM, D_MODEL), lambda i, j: (i, 0)),
            pl.BlockSpec((MLP_RES_ROWS, D_MODEL), lambda i, j: (i * MLP_STEPS + j, 0)),
            pl.BlockSpec((D_MODEL, MLP_TF), lambda i, j: (0, j)),
            pl.BlockSpec((MLP_TF, D_MODEL), lambda i, j: (j, 0)),
            pl.BlockSpec((1, D_MODEL), const),
            pl.BlockSpec((1, D_MODEL), const),
        ],
        out_specs=pl.BlockSpec((MLP_TM, D_MODEL), lambda i, j: (i, 0)),
        compiler_params=pltpu.CompilerParams(
            dimension_semantics=("arbitrary", "arbitrary"),
            vmem_limit_bytes=VMEM_LIMIT_BYTES),
        name="mlp_ln2",
    )(x1_bf, x1, w_up, w_down, g, b)


def _split_w_in(w_in):
    w_main = w_in[:, :SRC_AB].astype(BF16)
    w_z = w_in[:, SRC_Z:].astype(BF16)
    w_ab = jnp.pad(w_in[:, SRC_AB:SRC_Z], ((0, 0), (0, MXU_COLS - 2 * N_DELTA_HEADS))).astype(BF16)
    return w_main, w_z, w_ab


def _lane_row(v):
    return jnp.zeros((1, LANES), F32).at[0, :v.shape[0]].set(v.astype(F32))


def kernel(x, w_in, conv_w, a_log, dt_bias, delta_norm_w, attn_sinks, rel_bias, w_o, ln1_g, ln1_b,
           w_up, w_down, ln2_g, ln2_b):
    batch, seq, d = x.shape
    assert d == D_MODEL and seq % ATTN_BLOCK == 0 and (batch * seq) % IN_TM == 0
    assert w_in.shape[0] == DEPTH
    x2 = x.reshape(batch * seq, d)
    for l in range(DEPTH):
        proj, ab = _in_proj(x2, *_split_w_in(w_in[l]))
        attn_out = _swa_attention(proj, rel_bias.astype(F32), attn_sinks[l].astype(F32), batch, seq)
        delta_out = _gdn(proj, ab, conv_w[l].reshape(CONV_WIDTH, DELTA_QKV).astype(F32),
                         _lane_row(a_log[l]), _lane_row(dt_bias[l]),
                         delta_norm_w[l].reshape(1, DELTA_HEAD_DIM).astype(F32), batch, seq)
        x1, x1_bf = _out_ln1(x2, attn_out, delta_out, w_o[l].astype(BF16),
                             ln1_g[l].reshape(1, d), ln1_b[l].reshape(1, d))
        x2 = _mlp_ln2(x1, x1_bf, w_up[l].astype(BF16), w_down[l].astype(BF16),
                      ln2_g[l].reshape(1, d), ln2_b[l].reshape(1, d))
    return x2.reshape(batch, seq, d)
```

```python
import functools
import math

import jax
import jax.numpy as jnp
import numpy as np
from jax import lax
from jax.experimental import pallas as pl
from jax.experimental.pallas import tpu as pltpu

F32 = jnp.float32
BF16 = jnp.bfloat16

D_MODEL = 2048
ATTN_HEAD_DIM = 64
N_ATTN_HEADS = 16
N_KV_HEADS = 4
ATTN_BLOCK = 128
WINDOW = 128
NEG_INF = -1e30
N_BUCKETS = 32
MAX_DISTANCE = 128
DELTA_HEAD_DIM = 128
N_DELTA_HEADS = 8
DELTA_WIDTH = N_DELTA_HEADS * DELTA_HEAD_DIM
CONV_WIDTH = 4
CHUNK = 64
D_FF = 4 * D_MODEL
DEPTH = 1
DN_ALPHA = (2.0 * DEPTH) ** 0.25
LN_EPS = 1e-5
RMS_EPS = 1e-6

ATTN_Q = N_ATTN_HEADS * ATTN_HEAD_DIM
ATTN_KV = N_KV_HEADS * ATTN_HEAD_DIM
DELTA_QKV = 3 * DELTA_WIDTH

LANES = 128
SUBLANES = 8
VMEM_LIMIT_BYTES = 56 * 1024 * 1024

COL_DQKV = 0
COL_Z = COL_DQKV + DELTA_QKV
COL_Q = COL_Z + DELTA_WIDTH
COL_K = COL_Q + ATTN_Q
COL_V = COL_K + ATTN_KV
PROJ_COLS = COL_V + ATTN_KV

SRC_Q = 0
SRC_DQKV = ATTN_Q + 2 * ATTN_KV
SRC_AB = SRC_DQKV + DELTA_QKV
SRC_Z = SRC_AB + 2 * N_DELTA_HEADS
MXU_COLS = 256


def _dot(a, b):
    return jnp.dot(a, b, preferred_element_type=F32)


def _dot_nt(a, b):
    return lax.dot_general(a, b, (((1,), (1,)), ((), ())), preferred_element_type=F32)


def _dot_tn(a, b):
    return lax.dot_general(a, b, (((0,), (0,)), ((), ())), preferred_element_type=F32)


def _layer_norm(y, g, b):
    mu = jnp.mean(y, axis=-1, keepdims=True)
    yc = y - mu
    var = jnp.mean(yc * yc, axis=-1, keepdims=True)
    return yc * lax.rsqrt(var + LN_EPS) * g + b


IN_TM = 512
IN_CHUNK = 256


def _shift_rows(t, s):
    tiles = t.reshape(t.shape[0] // SUBLANES, SUBLANES, t.shape[1])
    rot = pltpu.roll(tiles, s, axis=1)
    sub = lax.broadcasted_iota(jnp.int32, tiles.shape, 1)
    return jnp.where(sub < s, pltpu.roll(rot, 1, axis=0), rot).reshape(t.shape)


def _in_proj_kernel(x_ref, wmain_ref, wz_ref, wab_ref, convw_ref, o_ref, ab_ref, tail_ref, *,
                    seq_tiles):
    i = pl.program_id(0)
    seq_start = (i % seq_tiles) == 0
    d = DELTA_HEAD_DIM

    @pl.when(i == 0)
    def _():
        tail_ref[...] = jnp.zeros(tail_ref.shape, F32)

    xb = x_ref[...].astype(BF16)

    def delta_qkv(c):
        def finish(r):
            prev = jnp.where(seq_start, 0.0, tail_ref[:, c:c + IN_CHUNK])
            tail_ref[:, c:c + IN_CHUNK] = r[IN_TM - SUBLANES:, :]
            xe = jnp.concatenate([prev, r], axis=0)
            w = [convw_ref[j:j + 1, c:c + IN_CHUNK] for j in range(CONV_WIDTH)]
            xe2 = _shift_rows(xe, 2)
            y = (w[3] * xe + w[1] * xe2 + _shift_rows(w[2] * xe + w[0] * xe2, 1))[SUBLANES:, :]
            y = y * jax.nn.sigmoid(y)
            if c < 2 * DELTA_WIDTH:
                scale = d ** -0.5 if c < DELTA_WIDTH else None
                parts = []
                for h in range(IN_CHUNK // d):
                    t = y[:, h * d:(h + 1) * d]
                    t = t * lax.rsqrt(jnp.sum(t * t, axis=-1, keepdims=True) + RMS_EPS)
                    parts.append(t * scale if scale is not None else t)
                y = jnp.concatenate(parts, axis=1)
            o_ref[:, COL_DQKV + c:COL_DQKV + c + IN_CHUNK] = y.astype(o_ref.dtype)
        return wmain_ref, SRC_DQKV + c, IN_CHUNK, finish

    def gate(c):
        def finish(r):
            o_ref[:, COL_Z + c:COL_Z + c + IN_CHUNK] = (r * jax.nn.sigmoid(r)).astype(o_ref.dtype)
        return wz_ref, c, IN_CHUNK, finish

    def attn_qkv(c):
        def finish(r):
            o_ref[:, COL_Q + c:COL_Q + c + IN_CHUNK] = r.astype(o_ref.dtype)
        return wmain_ref, SRC_Q + c, IN_CHUNK, finish

    def gate_logits():
        def finish(r):
            ab_ref[...] = r[:, 0:LANES]
        return wab_ref, 0, MXU_COLS, finish

    heavy = [delta_qkv(c) for c in range(0, DELTA_QKV, IN_CHUNK)]
    light = ([attn_qkv(c) for c in range(0, ATTN_Q + 2 * ATTN_KV, IN_CHUNK)]
             + [gate(c) for c in range(0, DELTA_WIDTH, IN_CHUNK)] + [gate_logits()])
    order = [t for pair in zip(heavy, light) for t in pair]
    order += heavy[len(light):] + light[len(heavy):]
    pending = None
    for task in order:
        w_ref, col, width, finish = task
        r = _dot(xb, w_ref[:, col:col + width])
        if pending is not None:
            pending[0](pending[1])
        pending = (finish, r)
    pending[0](pending[1])


def _in_proj(x2, w_main, w_z, w_ab, conv_w, seq):
    t = x2.shape[0]
    const = lambda i: (0, 0)
    resident = pl.Buffered(1)
    return pl.pallas_call(
        functools.partial(_in_proj_kernel, seq_tiles=seq // IN_TM),
        out_shape=(jax.ShapeDtypeStruct((t, PROJ_COLS), BF16),
                   jax.ShapeDtypeStruct((t, LANES), F32)),
        grid=(t // IN_TM,),
        in_specs=[
            pl.BlockSpec((IN_TM, D_MODEL), lambda i: (i, 0)),
            pl.BlockSpec(w_main.shape, const, pipeline_mode=resident),
            pl.BlockSpec(w_z.shape, const, pipeline_mode=resident),
            pl.BlockSpec(w_ab.shape, const, pipeline_mode=resident),
            pl.BlockSpec((CONV_WIDTH, DELTA_QKV), const),
        ],
        out_specs=(pl.BlockSpec((IN_TM, PROJ_COLS), lambda i: (i, 0)),
                   pl.BlockSpec((IN_TM, LANES), lambda i: (i, 0))),
        scratch_shapes=[pltpu.VMEM((SUBLANES, DELTA_QKV), F32)],
        compiler_params=pltpu.CompilerParams(
            dimension_semantics=("arbitrary",),
            vmem_limit_bytes=VMEM_LIMIT_BYTES),
        name="in_proj",
    )(x2, w_main, w_z, w_ab, conv_w)


def _t5_bucket_table():
    qi = np.arange(ATTN_BLOCK, dtype=np.int64)[:, None]
    kj = np.arange(ATTN_BLOCK, dtype=np.int64)[None, :]
    dist = np.where(kj <= qi, qi - kj, qi + ATTN_BLOCK - kj)
    assert dist.min() >= 0 and dist.max() < WINDOW
    max_exact = N_BUCKETS // 2
    nf = np.maximum(dist, 1).astype(np.float64)
    large = max_exact + (np.log(nf / max_exact) / math.log(MAX_DISTANCE / max_exact)
                         * (N_BUCKETS - max_exact)).astype(np.int64)
    large = np.minimum(large, N_BUCKETS - 1)
    return np.where(dist < max_exact, dist, large).astype(np.int32)


def _attn_kernel(bucket_ref, relb_ref, sink_ref, q_ref, kp_ref, kc_ref, vp_ref, vc_ref,
                 o_ref, bias_ref):
    b = pl.program_id(0)
    n = pl.program_id(1)
    row = lax.broadcasted_iota(jnp.int32, (ATTN_BLOCK, ATTN_BLOCK), 0)
    col = lax.broadcasted_iota(jnp.int32, (ATTN_BLOCK, ATTN_BLOCK), 1)
    in_cur = col <= row

    @pl.when((b == 0) & (n == 0))
    def _():
        bucket = bucket_ref[...]

        def head_body(h, carry):
            acc = jnp.zeros(bucket.shape, F32)
            for bk in range(N_BUCKETS):
                acc = jnp.where(bucket == bk, relb_ref[bk, h], acc)
            bias_ref[0, h] = acc
            bias_ref[1, h] = jnp.where(in_cur, acc, NEG_INF)
            return carry

        lax.fori_loop(0, N_ATTN_HEADS, head_body, 0)

    table = jnp.where(n == 0, 1, 0)
    lane = lax.broadcasted_iota(jnp.int32, (2 * ATTN_BLOCK, LANES), 1)
    lo = lane < ATTN_HEAD_DIM

    q = q_ref[...] * (ATTN_HEAD_DIM ** -0.5)
    kcat = jnp.concatenate([kp_ref[...], kc_ref[...]], axis=0).astype(F32)
    vcat = jnp.concatenate([vp_ref[...], vc_ref[...]], axis=0).astype(F32)

    k_ab, v_ab = [], []
    for m in range(N_KV_HEADS // 2):
        kg = kcat[:, m * LANES:(m + 1) * LANES]
        vg = vcat[:, m * LANES:(m + 1) * LANES]
        kg_sw = pltpu.roll(kg, ATTN_HEAD_DIM, axis=1)
        vg_sw = pltpu.roll(vg, ATTN_HEAD_DIM, axis=1)
        for t in range(2):
            k_src, k_alt = (kg, kg_sw) if t == 0 else (kg_sw, kg)
            v_src, v_alt = (vg, vg_sw) if t == 0 else (vg_sw, vg)
            k_ab.append((jnp.where(lo, k_src, 0.0).astype(BF16), jnp.where(lo, 0.0, k_alt).astype(BF16)))
            v_ab.append((jnp.where(lo, v_src, 0.0).astype(BF16), jnp.where(lo, 0.0, v_alt).astype(BF16)))

    heads = range(N_ATTN_HEADS)
    kv_of = [hd // (N_ATTN_HEADS // N_KV_HEADS) for hd in heads]
    scores = [_dot_nt(q[:, (hd // 2) * LANES:(hd // 2 + 1) * LANES], k_ab[kv_of[hd]][hd % 2])
              for hd in heads]
    logits = [jnp.where(in_cur, scores[hd][:, ATTN_BLOCK:], scores[hd][:, :ATTN_BLOCK])
              + bias_ref[table, hd] for hd in heads]
    mx = [jnp.maximum(jnp.max(logits[hd], axis=-1, keepdims=True), sink_ref[hd]) for hd in heads]
    pr = [jnp.exp(logits[hd] - mx[hd]) for hd in heads]
    inv = [1.0 / (jnp.sum(pr[hd], axis=-1, keepdims=True) + jnp.exp(sink_ref[hd] - mx[hd]))
           for hd in heads]
    pv = [_dot(jnp.concatenate([jnp.where(in_cur, 0.0, pr[hd]), jnp.where(in_cur, pr[hd], 0.0)],
                               axis=1).astype(BF16), v_ab[kv_of[hd]][hd % 2]) * inv[hd]
          for hd in heads]
    for p in range(N_ATTN_HEADS // 2):
        o_ref[:, p * LANES:(p + 1) * LANES] = (pv[2 * p] + pv[2 * p + 1]).astype(o_ref.dtype)


def _swa_attention(proj, rel_bias, sinks, batch, seq):
    nb = seq // ATTN_BLOCK
    bucket = jnp.asarray(_t5_bucket_table())
    q_blk = COL_Q // ATTN_Q
    k_blk = COL_K // ATTN_KV
    v_blk = COL_V // ATTN_KV

    def cur(col):
        return lambda b, n: (b * nb + n, col)

    def prev(col):
        return lambda b, n: (b * nb + jnp.maximum(n - 1, 0), col)

    smem = pl.BlockSpec(memory_space=pltpu.SMEM)
    return pl.pallas_call(
        _attn_kernel,
        out_shape=jax.ShapeDtypeStruct((batch * seq, ATTN_Q), BF16),
        grid=(batch, nb),
        in_specs=[
            pl.BlockSpec((ATTN_BLOCK, ATTN_BLOCK), lambda b, n: (0, 0)),
            smem, smem,
            pl.BlockSpec((ATTN_BLOCK, ATTN_Q), cur(q_blk)),
            pl.BlockSpec((ATTN_BLOCK, ATTN_KV), prev(k_blk)),
            pl.BlockSpec((ATTN_BLOCK, ATTN_KV), cur(k_blk)),
            pl.BlockSpec((ATTN_BLOCK, ATTN_KV), prev(v_blk)),
            pl.BlockSpec((ATTN_BLOCK, ATTN_KV), cur(v_blk)),
        ],
        out_specs=pl.BlockSpec((ATTN_BLOCK, ATTN_Q), lambda b, n: (b * nb + n, 0)),
        scratch_shapes=[pltpu.VMEM((2, N_ATTN_HEADS, ATTN_BLOCK, ATTN_BLOCK), F32)],
        compiler_params=pltpu.CompilerParams(
            dimension_semantics=("arbitrary", "arbitrary"),
            vmem_limit_bytes=VMEM_LIMIT_BYTES),
        name="swa_attn",
    )(bucket, rel_bias, sinks, proj, proj, proj, proj, proj)


GDN_CHUNKS = 4
GDN_TOK = GDN_CHUNKS * CHUNK


def _gdn_kernel(x_ref, gate_ref, ab_ref, alog_ref, dtb_ref, normw_ref, o_ref, state_ref):
    c = pl.program_id(1)

    @pl.when(c == 0)
    def _():
        state_ref[...] = jnp.zeros(state_ref.shape, F32)

    ab = ab_ref[...]
    g_all = -jnp.exp(alog_ref[...]) * jax.nn.softplus(ab + dtb_ref[...])
    beta_all = jax.nn.sigmoid(ab)
    row = lax.broadcasted_iota(jnp.int32, (CHUNK, CHUNK), 0)
    colm = lax.broadcasted_iota(jnp.int32, (CHUNK, CHUNK), 1)
    tril = row >= colm
    strict = row > colm
    ltri = tril.astype(F32)
    eye = (row == colm).astype(F32)
    normw = normw_ref[...]
    d = DELTA_HEAD_DIM

    items = [(ci, h) for ci in range(GDN_CHUNKS) for h in range(N_DELTA_HEADS)]
    idx = range(len(items))

    def rows(ci):
        return slice(ci * CHUNK, (ci + 1) * CHUNK)

    g_cum = [jnp.dot(ltri, g_all[rows(ci)], precision=lax.Precision.HIGHEST,
                     preferred_element_type=F32) for ci in range(GDN_CHUNKS)]
    g_cum_t = [lax.dot_general(g_all[rows(ci)], ltri, (((0,), (1,)), ((), ())),
                               precision=lax.Precision.HIGHEST, preferred_element_type=F32)
               for ci in range(GDN_CHUNKS)]

    q_bf = [x_ref[rows(ci), h * d:(h + 1) * d] for ci, h in items]
    k_bf = [x_ref[rows(ci), DELTA_WIDTH + h * d:DELTA_WIDTH + (h + 1) * d] for ci, h in items]
    q = [q_bf[i].astype(F32) for i in idx]
    k = [k_bf[i].astype(F32) for i in idx]
    v = [x_ref[rows(ci), 2 * DELTA_WIDTH + h * d:2 * DELTA_WIDTH + (h + 1) * d].astype(F32)
         for ci, h in items]
    beta = [beta_all[rows(ci), N_DELTA_HEADS + h:N_DELTA_HEADS + h + 1] for ci, h in items]
    gc = [g_cum[ci][:, h:h + 1] for ci, h in items]
    g_last = [gc[i][CHUNK - 1:CHUNK, :] for i in idx]
    exp_g = [jnp.exp(gc[i]) for i in idx]
    decay = [jnp.exp(jnp.where(tril, gc[i] - g_cum_t[ci][h:h + 1, :], -jnp.inf))
             for i, (ci, h) in enumerate(items)]

    kb = [k[i] * beta[i] for i in idx]
    kq = [_dot_nt(jnp.concatenate([kb[i].astype(BF16), q_bf[i]], axis=0), k_bf[i]) for i in idx]
    attn = [(kq[i][CHUNK:2 * CHUNK] * decay[i]).astype(BF16) for i in idx]

    nj = [jnp.where(strict, -(kq[i][0:CHUNK] * decay[i]), 0.0) for i in idx]
    pj = [eye + nj[i] for i in idx]
    nj = [_dot(nj[i].astype(BF16), nj[i].astype(BF16)) for i in idx]
    for _ in range(4):
        both = [_dot(jnp.concatenate([pj[i], nj[i]], axis=0).astype(BF16), nj[i].astype(BF16))
                for i in idx]
        pj = [pj[i] + both[i][0:CHUNK] for i in idx]
        nj = [both[i][CHUNK:2 * CHUNK] for i in idx]
    pj = [pj[i] + _dot(pj[i].astype(BF16), nj[i].astype(BF16)) for i in idx]

    sol = [_dot(pj[i].astype(BF16),
                jnp.concatenate([v[i] * beta[i], kb[i] * exp_g[i]], axis=1).astype(BF16))
           for i in idx]
    wq = [jnp.concatenate([sol[i][:, d:2 * d], q[i] * exp_g[i]], axis=0).astype(BF16) for i in idx]
    k_dec = [(k[i] * jnp.exp(g_last[i] - gc[i])).astype(BF16) for i in idx]

    state = [state_ref[h] for h in range(N_DELTA_HEADS)]
    heads = range(N_DELTA_HEADS)
    o = []
    for ci in range(GDN_CHUNKS):
        it = [ci * N_DELTA_HEADS + h for h in heads]
        ws = [_dot(wq[i], state[h].astype(BF16)) for h, i in enumerate(it)]
        v_new = [(sol[i][:, 0:d] - ws[h][0:CHUNK]).astype(BF16) for h, i in enumerate(it)]
        o += [ws[h][CHUNK:2 * CHUNK] + _dot(attn[i], v_new[h]) for h, i in enumerate(it)]
        state = [state[h] * jnp.exp(g_last[i]) + _dot_tn(k_dec[i], v_new[h]) for h, i in enumerate(it)]
    for h in heads:
        state_ref[h] = state[h]

    for i, (ci, h) in enumerate(items):
        oh = o[i] * lax.rsqrt(jnp.mean(o[i] * o[i], axis=-1, keepdims=True) + RMS_EPS) * normw
        gate = gate_ref[rows(ci), h * d:(h + 1) * d].astype(F32)
        o_ref[rows(ci), h * d:(h + 1) * d] = (oh * gate).astype(o_ref.dtype)


def _gdn(proj, ab, alog_row, dtb_row, normw_row, batch, seq):
    ns = seq // GDN_TOK

    def cur(col):
        return lambda b, c: (b * ns + c, col)

    const = lambda b, c: (0, 0)
    return pl.pallas_call(
        _gdn_kernel,
        out_shape=jax.ShapeDtypeStruct((batch * seq, DELTA_WIDTH), BF16),
        grid=(batch, ns),
        in_specs=[
            pl.BlockSpec((GDN_TOK, DELTA_QKV), cur(COL_DQKV // DELTA_QKV)),
            pl.BlockSpec((GDN_TOK, DELTA_WIDTH), cur(COL_Z // DELTA_WIDTH)),
            pl.BlockSpec((GDN_TOK, LANES), cur(0)),
            pl.BlockSpec((1, LANES), const),
            pl.BlockSpec((1, LANES), const),
            pl.BlockSpec((1, DELTA_HEAD_DIM), const),
        ],
        out_specs=pl.BlockSpec((GDN_TOK, DELTA_WIDTH), lambda b, c: (b * ns + c, 0)),
        scratch_shapes=[
            pltpu.VMEM((N_DELTA_HEADS, DELTA_HEAD_DIM, DELTA_HEAD_DIM), F32),
        ],
        compiler_params=pltpu.CompilerParams(
            dimension_semantics=("arbitrary", "arbitrary"),
            vmem_limit_bytes=VMEM_LIMIT_BYTES),
        name="gdn",
    )(proj, proj, ab, alog_row, dtb_row, normw_row)


OUT_TM = 512


def _out_ln1_kernel(x_ref, attn_ref, delta_ref, wo_ref, g_ref, b_ref, o_ref, ob_ref):
    mixed = (_dot(attn_ref[...], wo_ref[0:ATTN_Q, :])
             + _dot(delta_ref[...], wo_ref[ATTN_Q:ATTN_Q + DELTA_WIDTH, :]))
    y = _layer_norm(DN_ALPHA * x_ref[...] + mixed, g_ref[...], b_ref[...])
    o_ref[...] = y
    ob_ref[...] = y.astype(BF16)


def _out_ln1(x2, attn_out, delta_out, w_o, g, b):
    t = x2.shape[0]
    const = lambda i: (0, 0)
    tile = lambda i: (i, 0)
    return pl.pallas_call(
        _out_ln1_kernel,
        out_shape=(jax.ShapeDtypeStruct((t, D_MODEL), F32),
                   jax.ShapeDtypeStruct((t, D_MODEL), BF16)),
        grid=(t // OUT_TM,),
        in_specs=[
            pl.BlockSpec((OUT_TM, D_MODEL), tile),
            pl.BlockSpec((OUT_TM, ATTN_Q), tile),
            pl.BlockSpec((OUT_TM, DELTA_WIDTH), tile),
            pl.BlockSpec((ATTN_Q + DELTA_WIDTH, D_MODEL), const),
            pl.BlockSpec((1, D_MODEL), const),
            pl.BlockSpec((1, D_MODEL), const),
        ],
        out_specs=(pl.BlockSpec((OUT_TM, D_MODEL), tile),
                   pl.BlockSpec((OUT_TM, D_MODEL), tile)),
        compiler_params=pltpu.CompilerParams(
            dimension_semantics=("arbitrary",),
            vmem_limit_bytes=VMEM_LIMIT_BYTES),
        name="out_ln1",
    )(x2, attn_out, delta_out, w_o, g, b)


MLP_TM = 1024
MLP_TF = 1024
MLP_STEPS = D_FF // MLP_TF
MLP_RES_ROWS = MLP_TM // MLP_STEPS


def _mlp_ln2_kernel(xb_ref, xres_ref, wup_ref, wdn_ref, g_ref, b_ref, o_ref):
    j = pl.program_id(1)

    def accumulate(first):
        a = jnp.maximum(_dot(xb_ref[...], wup_ref[...]), 0.0)
        upd = _dot((a * a).astype(BF16), wdn_ref[...])
        if first:
            o_ref[...] = upd
        else:
            o_ref[...] += upd

    pl.when(j == 0)(functools.partial(accumulate, True))
    pl.when(j > 0)(functools.partial(accumulate, False))

    rows = pl.ds(pl.multiple_of(j * MLP_RES_ROWS, MLP_RES_ROWS), MLP_RES_ROWS)
    o_ref[rows, :] += DN_ALPHA * xres_ref[...]

    @pl.when(j == MLP_STEPS - 1)
    def _():
        o_ref[...] = _layer_norm(o_ref[...], g_ref[...], b_ref[...])


def _mlp_ln2(x1, x1_bf, w_up, w_down, g, b):
    t = x1.shape[0]
    const = lambda i, j: (0, 0)
    return pl.pallas_call(
        _mlp_ln2_kernel,
        out_shape=jax.ShapeDtypeStruct((t, D_MODEL), F32),
        grid=(t // MLP_TM, MLP_STEPS),
        in_specs=[
            pl.BlockSpec((MLP_TM, D_MODEL), lambda i, j: (i, 0)),
            pl.BlockSpec((MLP_RES_ROWS, D_MODEL), lambda i, j: (i * MLP_STEPS + j, 0)),
            pl.BlockSpec((D_MODEL, MLP_TF), lambda i, j: (0, j)),
            pl.BlockSpec((MLP_TF, D_MODEL), lambda i, j: (j, 0)),
            pl.BlockSpec((1, D_MODEL), const),
            pl.BlockSpec((1, D_MODEL), const),
        ],
        out_specs=pl.BlockSpec((MLP_TM, D_MODEL), lambda i, j: (i, 0)),
        compiler_params=pltpu.CompilerParams(
            dimension_semantics=("arbitrary", "arbitrary"),
            vmem_limit_bytes=VMEM_LIMIT_BYTES),
        name="mlp_ln2",
    )(x1_bf, x1, w_up, w_down, g, b)


def _split_w_in(w_in):
    w_main = w_in[:, :SRC_AB].astype(BF16)
    w_z = w_in[:, SRC_Z:].astype(BF16)
    w_ab = jnp.pad(w_in[:, SRC_AB:SRC_Z], ((0, 0), (0, MXU_COLS - 2 * N_DELTA_HEADS))).astype(BF16)
    return w_main, w_z, w_ab


def _lane_row(v):
    return jnp.zeros((1, LANES), F32).at[0, :v.shape[0]].set(v.astype(F32))


def kernel(x, w_in, conv_w, a_log, dt_bias, delta_norm_w, attn_sinks, rel_bias, w_o, ln1_g, ln1_b,
           w_up, w_down, ln2_g, ln2_b):
    batch, seq, d = x.shape
    assert d == D_MODEL and seq % ATTN_BLOCK == 0 and seq % IN_TM == 0 and seq % GDN_TOK == 0
    assert w_in.shape[0] == DEPTH
    x2 = x.reshape(batch * seq, d)
    for l in range(DEPTH):
        proj, ab = _in_proj(x2, *_split_w_in(w_in[l]),
                            conv_w[l].reshape(CONV_WIDTH, DELTA_QKV).astype(F32), seq)
        attn_out = _swa_attention(proj, rel_bias.astype(F32), attn_sinks[l].astype(F32), batch, seq)
        delta_out = _gdn(proj, ab, _lane_row(a_log[l]), _lane_row(dt_bias[l]),
                         delta_norm_w[l].reshape(1, DELTA_HEAD_DIM).astype(F32), batch, seq)
        x1, x1_bf = _out_ln1(x2, attn_out, delta_out, w_o[l].astype(BF16),
                             ln1_g[l].reshape(1, d), ln1_b[l].reshape(1, d))
        x2 = _mlp_ln2(x1, x1_bf, w_up[l].astype(BF16), w_down[l].astype(BF16),
                      ln2_g[l].reshape(1, d), ln2_b[l].reshape(1, d))
    return x2.reshape(batch, seq, d)
```

```python
import functools
import math

import jax
import jax.numpy as jnp
import numpy as np
from jax import lax
from jax.experimental import pallas as pl
from jax.experimental.pallas import tpu as pltpu

F32 = jnp.float32
BF16 = jnp.bfloat16

D_MODEL = 2048
ATTN_HEAD_DIM = 64
N_ATTN_HEADS = 16
N_KV_HEADS = 4
ATTN_BLOCK = 128
WINDOW = 128
NEG_INF = -1e30
N_BUCKETS = 32
MAX_DISTANCE = 128
DELTA_HEAD_DIM = 128
N_DELTA_HEADS = 8
DELTA_WIDTH = N_DELTA_HEADS * DELTA_HEAD_DIM
CONV_WIDTH = 4
CHUNK = 64
D_FF = 4 * D_MODEL
DEPTH = 1
DN_ALPHA = (2.0 * DEPTH) ** 0.25
LN_EPS = 1e-5
RMS_EPS = 1e-6

ATTN_Q = N_ATTN_HEADS * ATTN_HEAD_DIM
ATTN_KV = N_KV_HEADS * ATTN_HEAD_DIM
DELTA_QKV = 3 * DELTA_WIDTH

LANES = 128
SUBLANES = 8
VMEM_LIMIT_BYTES = 56 * 1024 * 1024

COL_DQKV = 0
COL_Z = COL_DQKV + DELTA_QKV
COL_Q = COL_Z + DELTA_WIDTH
COL_K = COL_Q + ATTN_Q
COL_V = COL_K + ATTN_KV
PROJ_COLS = COL_V + ATTN_KV

SRC_Q = 0
SRC_DQKV = ATTN_Q + 2 * ATTN_KV
SRC_AB = SRC_DQKV + DELTA_QKV
SRC_Z = SRC_AB + 2 * N_DELTA_HEADS
MXU_COLS = 256


def _dot(a, b):
    return jnp.dot(a, b, preferred_element_type=F32)


def _dot_nt(a, b):
    return lax.dot_general(a, b, (((1,), (1,)), ((), ())), preferred_element_type=F32)


def _dot_tn(a, b):
    return lax.dot_general(a, b, (((0,), (0,)), ((), ())), preferred_element_type=F32)


def _layer_norm(y, g, b):
    mu = jnp.mean(y, axis=-1, keepdims=True)
    yc = y - mu
    var = jnp.mean(yc * yc, axis=-1, keepdims=True)
    return yc * lax.rsqrt(var + LN_EPS) * g + b


IN_TM = 512
IN_CHUNK = 256
IN_ROWS = 64


def _shift_rows(t, s):
    tiles = t.reshape(t.shape[0] // SUBLANES, SUBLANES, t.shape[1])
    rot = pltpu.roll(tiles, s, axis=1)
    sub = lax.broadcasted_iota(jnp.int32, tiles.shape, 1)
    return jnp.where(sub < s, pltpu.roll(rot, 1, axis=0), rot).reshape(t.shape)


def _in_proj_kernel(x_ref, wmain_ref, wz_ref, wab_ref, convw_ref, o_ref, ab_ref, tail_ref, *,
                    seq_tiles):
    i = pl.program_id(0)
    seq_start = (i % seq_tiles) == 0
    d = DELTA_HEAD_DIM

    @pl.when(i == 0)
    def _():
        tail_ref[...] = jnp.zeros(tail_ref.shape, F32)

    xb = x_ref[...].astype(BF16)

    def delta_qkv(c):
        def finish(r):
            prev = jnp.where(seq_start, 0.0, tail_ref[:, c:c + IN_CHUNK])
            tail_ref[:, c:c + IN_CHUNK] = r[IN_TM - SUBLANES:, :]
            w = [convw_ref[j:j + 1, c:c + IN_CHUNK] for j in range(CONV_WIDTH)]
            for rb in range(0, IN_TM, IN_ROWS):
                head = prev if rb == 0 else r[rb - SUBLANES:rb, :]
                xe = jnp.concatenate([head, r[rb:rb + IN_ROWS, :]], axis=0)
                xe2 = _shift_rows(xe, 2)
                y = (w[3] * xe + w[1] * xe2 + _shift_rows(w[2] * xe + w[0] * xe2, 1))[SUBLANES:, :]
                y = y * jax.nn.sigmoid(y)
                if c < 2 * DELTA_WIDTH:
                    scale = d ** -0.5 if c < DELTA_WIDTH else None
                    parts = []
                    for h in range(IN_CHUNK // d):
                        t = y[:, h * d:(h + 1) * d]
                        t = t * lax.rsqrt(jnp.sum(t * t, axis=-1, keepdims=True) + RMS_EPS)
                        parts.append(t * scale if scale is not None else t)
                    y = jnp.concatenate(parts, axis=1)
                o_ref[rb:rb + IN_ROWS, COL_DQKV + c:COL_DQKV + c + IN_CHUNK] = y.astype(o_ref.dtype)
        return wmain_ref, SRC_DQKV + c, IN_CHUNK, finish

    def gate(c):
        def finish(r):
            o_ref[:, COL_Z + c:COL_Z + c + IN_CHUNK] = (r * jax.nn.sigmoid(r)).astype(o_ref.dtype)
        return wz_ref, c, IN_CHUNK, finish

    def attn_qkv(c):
        def finish(r):
            o_ref[:, COL_Q + c:COL_Q + c + IN_CHUNK] = r.astype(o_ref.dtype)
        return wmain_ref, SRC_Q + c, IN_CHUNK, finish

    def gate_logits():
        def finish(r):
            ab_ref[...] = r[:, 0:LANES]
        return wab_ref, 0, MXU_COLS, finish

    heavy = [delta_qkv(c) for c in range(0, DELTA_QKV, IN_CHUNK)]
    light = ([attn_qkv(c) for c in range(0, ATTN_Q + 2 * ATTN_KV, IN_CHUNK)]
             + [gate(c) for c in range(0, DELTA_WIDTH, IN_CHUNK)] + [gate_logits()])
    order = [t for pair in zip(heavy, light) for t in pair]
    order += heavy[len(light):] + light[len(heavy):]
    pending = None
    for task in order:
        w_ref, col, width, finish = task
        r = _dot(xb, w_ref[:, col:col + width])
        if pending is not None:
            pending[0](pending[1])
        pending = (finish, r)
    pending[0](pending[1])


def _in_proj(x2, w_main, w_z, w_ab, conv_w, seq):
    t = x2.shape[0]
    const = lambda i: (0, 0)
    resident = pl.Buffered(1)
    return pl.pallas_call(
        functools.partial(_in_proj_kernel, seq_tiles=seq // IN_TM),
        out_shape=(jax.ShapeDtypeStruct((t, PROJ_COLS), BF16),
                   jax.ShapeDtypeStruct((t, LANES), F32)),
        grid=(t // IN_TM,),
        in_specs=[
            pl.BlockSpec((IN_TM, D_MODEL), lambda i: (i, 0)),
            pl.BlockSpec(w_main.shape, const, pipeline_mode=resident),
            pl.BlockSpec(w_z.shape, const, pipeline_mode=resident),
            pl.BlockSpec(w_ab.shape, const, pipeline_mode=resident),
            pl.BlockSpec((CONV_WIDTH, DELTA_QKV), const),
        ],
        out_specs=(pl.BlockSpec((IN_TM, PROJ_COLS), lambda i: (i, 0)),
                   pl.BlockSpec((IN_TM, LANES), lambda i: (i, 0))),
        scratch_shapes=[pltpu.VMEM((SUBLANES, DELTA_QKV), F32)],
        compiler_params=pltpu.CompilerParams(
            dimension_semantics=("arbitrary",),
            vmem_limit_bytes=VMEM_LIMIT_BYTES),
        name="in_proj",
    )(x2, w_main, w_z, w_ab, conv_w)


def _t5_bucket_table():
    qi = np.arange(ATTN_BLOCK, dtype=np.int64)[:, None]
    kj = np.arange(ATTN_BLOCK, dtype=np.int64)[None, :]
    dist = np.where(kj <= qi, qi - kj, qi + ATTN_BLOCK - kj)
    assert dist.min() >= 0 and dist.max() < WINDOW
    max_exact = N_BUCKETS // 2
    nf = np.maximum(dist, 1).astype(np.float64)
    large = max_exact + (np.log(nf / max_exact) / math.log(MAX_DISTANCE / max_exact)
                         * (N_BUCKETS - max_exact)).astype(np.int64)
    large = np.minimum(large, N_BUCKETS - 1)
    return np.where(dist < max_exact, dist, large).astype(np.int32)


def _attn_kernel(bucket_ref, relb_ref, sink_ref, q_ref, kp_ref, kc_ref, vp_ref, vc_ref,
                 o_ref, bias_ref):
    b = pl.program_id(0)
    n = pl.program_id(1)
    row = lax.broadcasted_iota(jnp.int32, (ATTN_BLOCK, ATTN_BLOCK), 0)
    col = lax.broadcasted_iota(jnp.int32, (ATTN_BLOCK, ATTN_BLOCK), 1)
    in_cur = col <= row

    @pl.when((b == 0) & (n == 0))
    def _():
        bucket = bucket_ref[...]

        def head_body(h, carry):
            acc = jnp.zeros(bucket.shape, F32)
            for bk in range(N_BUCKETS):
                acc = jnp.where(bucket == bk, relb_ref[bk, h], acc)
            bias_ref[0, h] = acc
            bias_ref[1, h] = jnp.where(in_cur, acc, NEG_INF)
            return carry

        lax.fori_loop(0, N_ATTN_HEADS, head_body, 0)

    table = jnp.where(n == 0, 1, 0)
    lane = lax.broadcasted_iota(jnp.int32, (2 * ATTN_BLOCK, LANES), 1)
    lo = lane < ATTN_HEAD_DIM

    q = q_ref[...] * (ATTN_HEAD_DIM ** -0.5)
    kcat = jnp.concatenate([kp_ref[...], kc_ref[...]], axis=0).astype(F32)
    vcat = jnp.concatenate([vp_ref[...], vc_ref[...]], axis=0).astype(F32)

    k_ab, v_ab = [], []
    for m in range(N_KV_HEADS // 2):
        kg = kcat[:, m * LANES:(m + 1) * LANES]
        vg = vcat[:, m * LANES:(m + 1) * LANES]
        kg_sw = pltpu.roll(kg, ATTN_HEAD_DIM, axis=1)
        vg_sw = pltpu.roll(vg, ATTN_HEAD_DIM, axis=1)
        for t in range(2):
            k_src, k_alt = (kg, kg_sw) if t == 0 else (kg_sw, kg)
            v_src, v_alt = (vg, vg_sw) if t == 0 else (vg_sw, vg)
            k_ab.append((jnp.where(lo, k_src, 0.0).astype(BF16), jnp.where(lo, 0.0, k_alt).astype(BF16)))
            v_ab.append((jnp.where(lo, v_src, 0.0).astype(BF16), jnp.where(lo, 0.0, v_alt).astype(BF16)))

    heads = range(N_ATTN_HEADS)
    kv_of = [hd // (N_ATTN_HEADS // N_KV_HEADS) for hd in heads]
    scores = [_dot_nt(q[:, (hd // 2) * LANES:(hd // 2 + 1) * LANES], k_ab[kv_of[hd]][hd % 2])
              for hd in heads]
    logits = [jnp.where(in_cur, scores[hd][:, ATTN_BLOCK:], scores[hd][:, :ATTN_BLOCK])
              + bias_ref[table, hd] for hd in heads]
    mx = [jnp.maximum(jnp.max(logits[hd], axis=-1, keepdims=True), sink_ref[hd]) for hd in heads]
    pr = [jnp.exp(logits[hd] - mx[hd]) for hd in heads]
    inv = [1.0 / (jnp.sum(pr[hd], axis=-1, keepdims=True) + jnp.exp(sink_ref[hd] - mx[hd]))
           for hd in heads]
    pv = [_dot(jnp.concatenate([jnp.where(in_cur, 0.0, pr[hd]), jnp.where(in_cur, pr[hd], 0.0)],
                               axis=1).astype(BF16), v_ab[kv_of[hd]][hd % 2]) * inv[hd]
          for hd in heads]
    for p in range(N_ATTN_HEADS // 2):
        o_ref[:, p * LANES:(p + 1) * LANES] = (pv[2 * p] + pv[2 * p + 1]).astype(o_ref.dtype)


def _swa_attention(proj, rel_bias, sinks, batch, seq):
    nb = seq // ATTN_BLOCK
    bucket = jnp.asarray(_t5_bucket_table())
    q_blk = COL_Q // ATTN_Q
    k_blk = COL_K // ATTN_KV
    v_blk = COL_V // ATTN_KV

    def cur(col):
        return lambda b, n: (b * nb + n, col)

    def prev(col):
        return lambda b, n: (b * nb + jnp.maximum(n - 1, 0), col)

    smem = pl.BlockSpec(memory_space=pltpu.SMEM)
    return pl.pallas_call(
        _attn_kernel,
        out_shape=jax.ShapeDtypeStruct((batch * seq, ATTN_Q), BF16),
        grid=(batch, nb),
        in_specs=[
            pl.BlockSpec((ATTN_BLOCK, ATTN_BLOCK), lambda b, n: (0, 0)),
            smem, smem,
            pl.BlockSpec((ATTN_BLOCK, ATTN_Q), cur(q_blk)),
            pl.BlockSpec((ATTN_BLOCK, ATTN_KV), prev(k_blk)),
            pl.BlockSpec((ATTN_BLOCK, ATTN_KV), cur(k_blk)),
            pl.BlockSpec((ATTN_BLOCK, ATTN_KV), prev(v_blk)),
            pl.BlockSpec((ATTN_BLOCK, ATTN_KV), cur(v_blk)),
        ],
        out_specs=pl.BlockSpec((ATTN_BLOCK, ATTN_Q), lambda b, n: (b * nb + n, 0)),
        scratch_shapes=[pltpu.VMEM((2, N_ATTN_HEADS, ATTN_BLOCK, ATTN_BLOCK), F32)],
        compiler_params=pltpu.CompilerParams(
            dimension_semantics=("arbitrary", "arbitrary"),
            vmem_limit_bytes=VMEM_LIMIT_BYTES),
        name="swa_attn",
    )(bucket, rel_bias, sinks, proj, proj, proj, proj, proj)


GDN_CHUNKS = 4
GDN_TOK = GDN_CHUNKS * CHUNK
GDN_GROUP = MXU_COLS // CHUNK


def _gdn_kernel(x_ref, gate_ref, ab_ref, alog_ref, dtb_ref, normw_ref, o_ref, state_ref):
    c = pl.program_id(1)

    @pl.when(c == 0)
    def _():
        state_ref[...] = jnp.zeros(state_ref.shape, F32)

    ab = ab_ref[...]
    g_all = -jnp.exp(alog_ref[...]) * jax.nn.softplus(ab + dtb_ref[...])
    beta_all = jax.nn.sigmoid(ab)
    row = lax.broadcasted_iota(jnp.int32, (CHUNK, CHUNK), 0)
    colm = lax.broadcasted_iota(jnp.int32, (CHUNK, CHUNK), 1)
    tril = row >= colm
    strict = row > colm
    ltri = tril.astype(F32)
    eye = (row == colm).astype(F32)
    normw = normw_ref[...]
    d = DELTA_HEAD_DIM

    items = [(ci, h) for ci in range(GDN_CHUNKS) for h in range(N_DELTA_HEADS)]
    idx = range(len(items))

    def rows(ci):
        return slice(ci * CHUNK, (ci + 1) * CHUNK)

    g_cum = [jnp.dot(ltri, g_all[rows(ci)], precision=lax.Precision.HIGHEST,
                     preferred_element_type=F32) for ci in range(GDN_CHUNKS)]
    g_cum_t = [lax.dot_general(g_all[rows(ci)], ltri, (((0,), (1,)), ((), ())),
                               precision=lax.Precision.HIGHEST, preferred_element_type=F32)
               for ci in range(GDN_CHUNKS)]

    q_bf = [x_ref[rows(ci), h * d:(h + 1) * d] for ci, h in items]
    k_bf = [x_ref[rows(ci), DELTA_WIDTH + h * d:DELTA_WIDTH + (h + 1) * d] for ci, h in items]
    q = [q_bf[i].astype(F32) for i in idx]
    k = [k_bf[i].astype(F32) for i in idx]
    v = [x_ref[rows(ci), 2 * DELTA_WIDTH + h * d:2 * DELTA_WIDTH + (h + 1) * d].astype(F32)
         for ci, h in items]
    beta = [beta_all[rows(ci), N_DELTA_HEADS + h:N_DELTA_HEADS + h + 1] for ci, h in items]
    gc = [g_cum[ci][:, h:h + 1] for ci, h in items]
    g_last = [gc[i][CHUNK - 1:CHUNK, :] for i in idx]
    exp_g = [jnp.exp(gc[i]) for i in idx]
    decay = [jnp.exp(jnp.where(tril, gc[i] - g_cum_t[ci][h:h + 1, :], -jnp.inf))
             for i, (ci, h) in enumerate(items)]

    kb = [k[i] * beta[i] for i in idx]
    kq = [_dot_nt(jnp.concatenate([kb[i].astype(BF16), q_bf[i]], axis=0), k_bf[i]) for i in idx]
    attn = [(kq[i][CHUNK:2 * CHUNK] * decay[i]).astype(BF16) for i in idx]

    groups = [list(range(s, s + GDN_GROUP)) for s in range(0, len(items), GDN_GROUP)]
    gidx = range(len(groups))
    wide = (CHUNK, GDN_GROUP * CHUNK)
    lane_blk = lax.broadcasted_iota(jnp.int32, wide, 1) // CHUNK
    eye_cat = (lax.broadcasted_iota(jnp.int32, wide, 0)
               == lax.broadcasted_iota(jnp.int32, wide, 1) % CHUNK).astype(F32)

    def block_diag(n_cat):
        return jnp.concatenate([jnp.where(lane_blk == p, n_cat, 0.0) for p in range(GDN_GROUP)],
                               axis=0).astype(BF16)

    nj = [jnp.concatenate([jnp.where(strict, -(kq[i][0:CHUNK] * decay[i]), 0.0) for i in grp], axis=1)
          for grp in groups]
    pj = [eye_cat + nj[g] for g in gidx]
    nj = [_dot(nj[g].astype(BF16), block_diag(nj[g])) for g in gidx]
    for _ in range(4):
        both = [_dot(jnp.concatenate([pj[g], nj[g]], axis=0).astype(BF16), block_diag(nj[g]))
                for g in gidx]
        pj = [pj[g] + both[g][0:CHUNK] for g in gidx]
        nj = [both[g][CHUNK:2 * CHUNK] for g in gidx]
    pj = [pj[g] + _dot(pj[g].astype(BF16), block_diag(nj[g])) for g in gidx]
    pj = [pj[g][:, p * CHUNK:(p + 1) * CHUNK] for g in gidx for p in range(GDN_GROUP)]

    sol = [_dot(pj[i].astype(BF16),
                jnp.concatenate([v[i] * beta[i], kb[i] * exp_g[i]], axis=1).astype(BF16))
           for i in idx]
    wq = [jnp.concatenate([sol[i][:, d:2 * d], q[i] * exp_g[i]], axis=0).astype(BF16) for i in idx]
    k_dec = [(k[i] * jnp.exp(g_last[i] - gc[i])).astype(BF16) for i in idx]

    state = [state_ref[h] for h in range(N_DELTA_HEADS)]
    heads = range(N_DELTA_HEADS)
    o = []
    for ci in range(GDN_CHUNKS):
        it = [ci * N_DELTA_HEADS + h for h in heads]
        ws = [_dot(wq[i], state[h].astype(BF16)) for h, i in enumerate(it)]
        v_new = [(sol[i][:, 0:d] - ws[h][0:CHUNK]).astype(BF16) for h, i in enumerate(it)]
        o += [ws[h][CHUNK:2 * CHUNK] + _dot(attn[i], v_new[h]) for h, i in enumerate(it)]
        state = [state[h] * jnp.exp(g_last[i]) + _dot_tn(k_dec[i], v_new[h]) for h, i in enumerate(it)]
    for h in heads:
        state_ref[h] = state[h]

    for i, (ci, h) in enumerate(items):
        oh = o[i] * lax.rsqrt(jnp.mean(o[i] * o[i], axis=-1, keepdims=True) + RMS_EPS) * normw
        gate = gate_ref[rows(ci), h * d:(h + 1) * d].astype(F32)
        o_ref[rows(ci), h * d:(h + 1) * d] = (oh * gate).astype(o_ref.dtype)


def _gdn(proj, ab, alog_row, dtb_row, normw_row, batch, seq):
    ns = seq // GDN_TOK

    def cur(col):
        return lambda b, c: (b * ns + c, col)

    const = lambda b, c: (0, 0)
    return pl.pallas_call(
        _gdn_kernel,
        out_shape=jax.ShapeDtypeStruct((batch * seq, DELTA_WIDTH), BF16),
        grid=(batch, ns),
        in_specs=[
            pl.BlockSpec((GDN_TOK, DELTA_QKV), cur(COL_DQKV // DELTA_QKV)),
            pl.BlockSpec((GDN_TOK, DELTA_WIDTH), cur(COL_Z // DELTA_WIDTH)),
            pl.BlockSpec((GDN_TOK, LANES), cur(0)),
            pl.BlockSpec((1, LANES), const),
            pl.BlockSpec((1, LANES), const),
            pl.BlockSpec((1, DELTA_HEAD_DIM), const),
        ],
        out_specs=pl.BlockSpec((GDN_TOK, DELTA_WIDTH), lambda b, c: (b * ns + c, 0)),
        scratch_shapes=[
            pltpu.VMEM((N_DELTA_HEADS, DELTA_HEAD_DIM, DELTA_HEAD_DIM), F32),
        ],
        compiler_params=pltpu.CompilerParams(
            dimension_semantics=("arbitrary", "arbitrary"),
            vmem_limit_bytes=VMEM_LIMIT_BYTES),
        name="gdn",
    )(proj, proj, ab, alog_row, dtb_row, normw_row)


OUT_TM = 512


def _out_ln1_kernel(x_ref, attn_ref, delta_ref, wo_ref, g_ref, b_ref, o_ref, ob_ref):
    mixed = (_dot(attn_ref[...], wo_ref[0:ATTN_Q, :])
             + _dot(delta_ref[...], wo_ref[ATTN_Q:ATTN_Q + DELTA_WIDTH, :]))
    y = _layer_norm(DN_ALPHA * x_ref[...] + mixed, g_ref[...], b_ref[...])
    o_ref[...] = y
    ob_ref[...] = y.astype(BF16)


def _out_ln1(x2, attn_out, delta_out, w_o, g, b):
    t = x2.shape[0]
    const = lambda i: (0, 0)
    tile = lambda i: (i, 0)
    return pl.pallas_call(
        _out_ln1_kernel,
        out_shape=(jax.ShapeDtypeStruct((t, D_MODEL), F32),
                   jax.ShapeDtypeStruct((t, D_MODEL), BF16)),
        grid=(t // OUT_TM,),
        in_specs=[
            pl.BlockSpec((OUT_TM, D_MODEL), tile),
            pl.BlockSpec((OUT_TM, ATTN_Q), tile),
            pl.BlockSpec((OUT_TM, DELTA_WIDTH), tile),
            pl.BlockSpec((ATTN_Q + DELTA_WIDTH, D_MODEL), const),
            pl.BlockSpec((1, D_MODEL), const),
            pl.BlockSpec((1, D_MODEL), const),
        ],
        out_specs=(pl.BlockSpec((OUT_TM, D_MODEL), tile),
                   pl.BlockSpec((OUT_TM, D_MODEL), tile)),
        compiler_params=pltpu.CompilerParams(
            dimension_semantics=("arbitrary",),
            vmem_limit_bytes=VMEM_LIMIT_BYTES),
        name="out_ln1",
    )(x2, attn_out, delta_out, w_o, g, b)


MLP_TM = 1024
MLP_TF = 1024
MLP_STEPS = D_FF // MLP_TF
MLP_RES_ROWS = MLP_TM // MLP_STEPS


def _mlp_ln2_kernel(xb_ref, xres_ref, wup_ref, wdn_ref, g_ref, b_ref, o_ref):
    j = pl.program_id(1)

    def accumulate(first):
        a = jnp.maximum(_dot(xb_ref[...], wup_ref[...]), 0.0)
        upd = _dot((a * a).astype(BF16), wdn_ref[...])
        if first:
            o_ref[...] = upd
        else:
            o_ref[...] += upd

    pl.when(j == 0)(functools.partial(accumulate, True))
    pl.when(j > 0)(functools.partial(accumulate, False))

    rows = pl.ds(pl.multiple_of(j * MLP_RES_ROWS, MLP_RES_ROWS), MLP_RES_ROWS)
    o_ref[rows, :] += DN_ALPHA * xres_ref[...]

    @pl.when(j == MLP_STEPS - 1)
    def _():
        o_ref[...] = _layer_norm(o_ref[...], g_ref[...], b_ref[...])


def _mlp_ln2(x1, x1_bf, w_up, w_down, g, b):
    t = x1.shape[0]
    const = lambda i, j: (0, 0)
    return pl.pallas_call(
        _mlp_ln2_kernel,
        out_shape=jax.ShapeDtypeStruct((t, D_MODEL), F32),
        grid=(t // MLP_TM, MLP_STEPS),
        in_specs=[
            pl.BlockSpec((MLP_TM, D_MODEL), lambda i, j: (i, 0)),
            pl.BlockSpec((MLP_RES_ROWS, D_MODEL), lambda i, j: (i * MLP_STEPS + j, 0)),
            pl.BlockSpec((D_MODEL, MLP_TF), lambda i, j: (0, j)),
            pl.BlockSpec((MLP_TF, D_MODEL), lambda i, j: (j, 0)),
            pl.BlockSpec((1, D_MODEL), const),
            pl.BlockSpec((1, D_MODEL), const),
        ],
        out_specs=pl.BlockSpec((MLP_TM, D_MODEL), lambda i, j: (i, 0)),
        compiler_params=pltpu.CompilerParams(
            dimension_semantics=("arbitrary", "arbitrary"),
            vmem_limit_bytes=VMEM_LIMIT_BYTES),
        name="mlp_ln2",
    )(x1_bf, x1, w_up, w_down, g, b)


def _split_w_in(w_in):
    w_main = w_in[:, :SRC_AB].astype(BF16)
    w_z = w_in[:, SRC_Z:].astype(BF16)
    w_ab = jnp.pad(w_in[:, SRC_AB:SRC_Z], ((0, 0), (0, MXU_COLS - 2 * N_DELTA_HEADS))).astype(BF16)
    return w_main, w_z, w_ab


def _lane_row(v):
    return jnp.zeros((1, LANES), F32).at[0, :v.shape[0]].set(v.astype(F32))


def kernel(x, w_in, conv_w, a_log, dt_bias, delta_norm_w, attn_sinks, rel_bias, w_o, ln1_g, ln1_b,
           w_up, w_down, ln2_g, ln2_b):
    batch, seq, d = x.shape
    assert d == D_MODEL and seq % ATTN_BLOCK == 0 and seq % IN_TM == 0 and seq % GDN_TOK == 0
    assert w_in.shape[0] == DEPTH
    x2 = x.reshape(batch * seq, d)
    for l in range(DEPTH):
        proj, ab = _in_proj(x2, *_split_w_in(w_in[l]),
                            conv_w[l].reshape(CONV_WIDTH, DELTA_QKV).astype(F32), seq)
        attn_out = _swa_attention(proj, rel_bias.astype(F32), attn_sinks[l].astype(F32), batch, seq)
        delta_out = _gdn(proj, ab, _lane_row(a_log[l]), _lane_row(dt_bias[l]),
                         delta_norm_w[l].reshape(1, DELTA_HEAD_DIM).astype(F32), batch, seq)
        x1, x1_bf = _out_ln1(x2, attn_out, delta_out, w_o[l].astype(BF16),
                             ln1_g[l].reshape(1, d), ln1_b[l].reshape(1, d))
        x2 = _mlp_ln2(x1, x1_bf, w_up[l].astype(BF16), w_down[l].astype(BF16),
                      ln2_g[l].reshape(1, d), ln2_b[l].reshape(1, d))
    return x2.reshape(batch, seq, d)
```

```python
import functools
import math

import jax
import jax.numpy as jnp
import numpy as np
from jax import lax
from jax.experimental import pallas as pl
from jax.experimental.pallas import tpu as pltpu

F32 = jnp.float32
BF16 = jnp.bfloat16

D_MODEL = 2048
ATTN_HEAD_DIM = 64
N_ATTN_HEADS = 16
N_KV_HEADS = 4
ATTN_BLOCK = 128
WINDOW = 128
NEG_INF = -1e30
N_BUCKETS = 32
MAX_DISTANCE = 128
DELTA_HEAD_DIM = 128
N_DELTA_HEADS = 8
DELTA_WIDTH = N_DELTA_HEADS * DELTA_HEAD_DIM
CONV_WIDTH = 4
CHUNK = 64
D_FF = 4 * D_MODEL
DEPTH = 1
DN_ALPHA = (2.0 * DEPTH) ** 0.25
LN_EPS = 1e-5
RMS_EPS = 1e-6

ATTN_Q = N_ATTN_HEADS * ATTN_HEAD_DIM
ATTN_KV = N_KV_HEADS * ATTN_HEAD_DIM
DELTA_QKV = 3 * DELTA_WIDTH

LANES = 128
SUBLANES = 8
VMEM_LIMIT_BYTES = 56 * 1024 * 1024

COL_DQKV = 0
COL_Z = COL_DQKV + DELTA_QKV
COL_Q = COL_Z + DELTA_WIDTH
COL_K = COL_Q + ATTN_Q
COL_V = COL_K + ATTN_KV
PROJ_COLS = COL_V + ATTN_KV

SRC_Q = 0
SRC_DQKV = ATTN_Q + 2 * ATTN_KV
SRC_AB = SRC_DQKV + DELTA_QKV
SRC_Z = SRC_AB + 2 * N_DELTA_HEADS
MXU_COLS = 256


def _dot(a, b):
    return jnp.dot(a, b, preferred_element_type=F32)


def _dot_nt(a, b):
    return lax.dot_general(a, b, (((1,), (1,)), ((), ())), preferred_element_type=F32)


def _dot_tn(a, b):
    return lax.dot_general(a, b, (((0,), (0,)), ((), ())), preferred_element_type=F32)


def _layer_norm(y, g, b):
    mu = jnp.mean(y, axis=-1, keepdims=True)
    yc = y - mu
    var = jnp.mean(yc * yc, axis=-1, keepdims=True)
    return yc * lax.rsqrt(var + LN_EPS) * g + b


IN_TM = 512
IN_CHUNK = 256
IN_ROWS = 64


def _shift_rows(t, s):
    tiles = t.reshape(t.shape[0] // SUBLANES, SUBLANES, t.shape[1])
    rot = pltpu.roll(tiles, s, axis=1)
    sub = lax.broadcasted_iota(jnp.int32, tiles.shape, 1)
    return jnp.where(sub < s, pltpu.roll(rot, 1, axis=0), rot).reshape(t.shape)


def _in_proj_kernel(x_ref, wmain_ref, wz_ref, wab_ref, convw_ref, o_ref, ab_ref, tail_ref, *,
                    seq_tiles):
    i = pl.program_id(0)
    seq_start = (i % seq_tiles) == 0
    d = DELTA_HEAD_DIM

    @pl.when(i == 0)
    def _():
        tail_ref[...] = jnp.zeros(tail_ref.shape, F32)

    xb = x_ref[...].astype(BF16)

    def delta_qkv(c):
        def finish(r):
            prev = jnp.where(seq_start, 0.0, tail_ref[:, c:c + IN_CHUNK])
            tail_ref[:, c:c + IN_CHUNK] = r[IN_TM - SUBLANES:, :]
            w = [convw_ref[j:j + 1, c:c + IN_CHUNK] for j in range(CONV_WIDTH)]
            for rb in range(0, IN_TM, IN_ROWS):
                head = prev if rb == 0 else r[rb - SUBLANES:rb, :]
                xe = jnp.concatenate([head, r[rb:rb + IN_ROWS, :]], axis=0)
                xe2 = _shift_rows(xe, 2)
                y = (w[3] * xe + w[1] * xe2 + _shift_rows(w[2] * xe + w[0] * xe2, 1))[SUBLANES:, :]
                y = y * jax.nn.sigmoid(y)
                if c < 2 * DELTA_WIDTH:
                    scale = d ** -0.5 if c < DELTA_WIDTH else None
                    parts = []
                    for h in range(IN_CHUNK // d):
                        t = y[:, h * d:(h + 1) * d]
                        t = t * lax.rsqrt(jnp.sum(t * t, axis=-1, keepdims=True) + RMS_EPS)
                        parts.append(t * scale if scale is not None else t)
                    y = jnp.concatenate(parts, axis=1)
                o_ref[rb:rb + IN_ROWS, COL_DQKV + c:COL_DQKV + c + IN_CHUNK] = y.astype(o_ref.dtype)
        return wmain_ref, SRC_DQKV + c, IN_CHUNK, finish

    def gate(c):
        def finish(r):
            o_ref[:, COL_Z + c:COL_Z + c + IN_CHUNK] = (r * jax.nn.sigmoid(r)).astype(o_ref.dtype)
        return wz_ref, c, IN_CHUNK, finish

    def attn_qkv(c):
        def finish(r):
            o_ref[:, COL_Q + c:COL_Q + c + IN_CHUNK] = r.astype(o_ref.dtype)
        return wmain_ref, SRC_Q + c, IN_CHUNK, finish

    def gate_logits():
        def finish(r):
            ab_ref[...] = r[:, 0:LANES]
        return wab_ref, 0, MXU_COLS, finish

    heavy = [delta_qkv(c) for c in range(0, DELTA_QKV, IN_CHUNK)]
    light = ([attn_qkv(c) for c in range(0, ATTN_Q + 2 * ATTN_KV, IN_CHUNK)]
             + [gate(c) for c in range(0, DELTA_WIDTH, IN_CHUNK)] + [gate_logits()])
    order = [t for pair in zip(heavy, light) for t in pair]
    order += heavy[len(light):] + light[len(heavy):]
    pending = None
    for task in order:
        w_ref, col, width, finish = task
        r = _dot(xb, w_ref[:, col:col + width])
        if pending is not None:
            pending[0](pending[1])
        pending = (finish, r)
    pending[0](pending[1])


def _in_proj(x2, w_main, w_z, w_ab, conv_w, seq):
    t = x2.shape[0]
    const = lambda i: (0, 0)
    resident = pl.Buffered(1)
    return pl.pallas_call(
        functools.partial(_in_proj_kernel, seq_tiles=seq // IN_TM),
        out_shape=(jax.ShapeDtypeStruct((t, PROJ_COLS), BF16),
                   jax.ShapeDtypeStruct((t, LANES), F32)),
        grid=(t // IN_TM,),
        in_specs=[
            pl.BlockSpec((IN_TM, D_MODEL), lambda i: (i, 0)),
            pl.BlockSpec(w_main.shape, const, pipeline_mode=resident),
            pl.BlockSpec(w_z.shape, const, pipeline_mode=resident),
            pl.BlockSpec(w_ab.shape, const, pipeline_mode=resident),
            pl.BlockSpec((CONV_WIDTH, DELTA_QKV), const),
        ],
        out_specs=(pl.BlockSpec((IN_TM, PROJ_COLS), lambda i: (i, 0)),
                   pl.BlockSpec((IN_TM, LANES), lambda i: (i, 0))),
        scratch_shapes=[pltpu.VMEM((SUBLANES, DELTA_QKV), F32)],
        compiler_params=pltpu.CompilerParams(
            dimension_semantics=("arbitrary",),
            vmem_limit_bytes=VMEM_LIMIT_BYTES),
        name="in_proj",
    )(x2, w_main, w_z, w_ab, conv_w)


def _t5_bucket_table():
    qi = np.arange(ATTN_BLOCK, dtype=np.int64)[:, None]
    kj = np.arange(ATTN_BLOCK, dtype=np.int64)[None, :]
    dist = np.where(kj <= qi, qi - kj, qi + ATTN_BLOCK - kj)
    assert dist.min() >= 0 and dist.max() < WINDOW
    max_exact = N_BUCKETS // 2
    nf = np.maximum(dist, 1).astype(np.float64)
    large = max_exact + (np.log(nf / max_exact) / math.log(MAX_DISTANCE / max_exact)
                         * (N_BUCKETS - max_exact)).astype(np.int64)
    large = np.minimum(large, N_BUCKETS - 1)
    return np.where(dist < max_exact, dist, large).astype(np.int32)


def _attn_kernel(bucket_ref, relb_ref, sink_ref, q_ref, kp_ref, kc_ref, vp_ref, vc_ref,
                 o_ref, bias_ref):
    b = pl.program_id(0)
    n = pl.program_id(1)
    row = lax.broadcasted_iota(jnp.int32, (ATTN_BLOCK, ATTN_BLOCK), 0)
    col = lax.broadcasted_iota(jnp.int32, (ATTN_BLOCK, ATTN_BLOCK), 1)
    in_cur = col <= row

    @pl.when((b == 0) & (n == 0))
    def _():
        bucket = bucket_ref[...]

        def head_body(h, carry):
            acc = jnp.zeros(bucket.shape, F32)
            for bk in range(N_BUCKETS):
                acc = jnp.where(bucket == bk, relb_ref[bk, h], acc)
            bias_ref[0, h] = acc
            bias_ref[1, h] = jnp.where(in_cur, acc, NEG_INF)
            return carry

        lax.fori_loop(0, N_ATTN_HEADS, head_body, 0)

    table = jnp.where(n == 0, 1, 0)
    lane = lax.broadcasted_iota(jnp.int32, (2 * ATTN_BLOCK, LANES), 1)
    lo = lane < ATTN_HEAD_DIM

    q = q_ref[...] * (ATTN_HEAD_DIM ** -0.5)
    kcat = jnp.concatenate([kp_ref[...], kc_ref[...]], axis=0).astype(F32)
    vcat = jnp.concatenate([vp_ref[...], vc_ref[...]], axis=0).astype(F32)

    k_ab, v_ab = [], []
    for m in range(N_KV_HEADS // 2):
        kg = kcat[:, m * LANES:(m + 1) * LANES]
        vg = vcat[:, m * LANES:(m + 1) * LANES]
        kg_sw = pltpu.roll(kg, ATTN_HEAD_DIM, axis=1)
        vg_sw = pltpu.roll(vg, ATTN_HEAD_DIM, axis=1)
        for t in range(2):
            k_src, k_alt = (kg, kg_sw) if t == 0 else (kg_sw, kg)
            v_src, v_alt = (vg, vg_sw) if t == 0 else (vg_sw, vg)
            k_ab.append((jnp.where(lo, k_src, 0.0).astype(BF16), jnp.where(lo, 0.0, k_alt).astype(BF16)))
            v_ab.append((jnp.where(lo, v_src, 0.0).astype(BF16), jnp.where(lo, 0.0, v_alt).astype(BF16)))

    heads = range(N_ATTN_HEADS)
    kv_of = [hd // (N_ATTN_HEADS // N_KV_HEADS) for hd in heads]
    scores = [_dot_nt(q[:, (hd // 2) * LANES:(hd // 2 + 1) * LANES], k_ab[kv_of[hd]][hd % 2])
              for hd in heads]
    logits = [jnp.where(in_cur, scores[hd][:, ATTN_BLOCK:], scores[hd][:, :ATTN_BLOCK])
              + bias_ref[table, hd] for hd in heads]
    mx = [jnp.maximum(jnp.max(logits[hd], axis=-1, keepdims=True), sink_ref[hd]) for hd in heads]
    pr = [jnp.exp(logits[hd] - mx[hd]) for hd in heads]
    inv = [1.0 / (jnp.sum(pr[hd], axis=-1, keepdims=True) + jnp.exp(sink_ref[hd] - mx[hd]))
           for hd in heads]
    pv = [_dot(jnp.concatenate([jnp.where(in_cur, 0.0, pr[hd]), jnp.where(in_cur, pr[hd], 0.0)],
                               axis=1).astype(BF16), v_ab[kv_of[hd]][hd % 2]) * inv[hd]
          for hd in heads]
    for p in range(N_ATTN_HEADS // 2):
        o_ref[:, p * LANES:(p + 1) * LANES] = (pv[2 * p] + pv[2 * p + 1]).astype(o_ref.dtype)


def _swa_attention(proj, rel_bias, sinks, batch, seq):
    nb = seq // ATTN_BLOCK
    bucket = jnp.asarray(_t5_bucket_table())
    q_blk = COL_Q // ATTN_Q
    k_blk = COL_K // ATTN_KV
    v_blk = COL_V // ATTN_KV

    def cur(col):
        return lambda b, n: (b * nb + n, col)

    def prev(col):
        return lambda b, n: (b * nb + jnp.maximum(n - 1, 0), col)

    smem = pl.BlockSpec(memory_space=pltpu.SMEM)
    return pl.pallas_call(
        _attn_kernel,
        out_shape=jax.ShapeDtypeStruct((batch * seq, ATTN_Q), BF16),
        grid=(batch, nb),
        in_specs=[
            pl.BlockSpec((ATTN_BLOCK, ATTN_BLOCK), lambda b, n: (0, 0)),
            smem, smem,
            pl.BlockSpec((ATTN_BLOCK, ATTN_Q), cur(q_blk)),
            pl.BlockSpec((ATTN_BLOCK, ATTN_KV), prev(k_blk)),
            pl.BlockSpec((ATTN_BLOCK, ATTN_KV), cur(k_blk)),
            pl.BlockSpec((ATTN_BLOCK, ATTN_KV), prev(v_blk)),
            pl.BlockSpec((ATTN_BLOCK, ATTN_KV), cur(v_blk)),
        ],
        out_specs=pl.BlockSpec((ATTN_BLOCK, ATTN_Q), lambda b, n: (b * nb + n, 0)),
        scratch_shapes=[pltpu.VMEM((2, N_ATTN_HEADS, ATTN_BLOCK, ATTN_BLOCK), F32)],
        compiler_params=pltpu.CompilerParams(
            dimension_semantics=("arbitrary", "arbitrary"),
            vmem_limit_bytes=VMEM_LIMIT_BYTES),
        name="swa_attn",
    )(bucket, rel_bias, sinks, proj, proj, proj, proj, proj)


GDN_CHUNKS = 8
GDN_TOK = GDN_CHUNKS * CHUNK
GDN_GROUP = MXU_COLS // CHUNK


def _gdn_kernel(x_ref, gate_ref, ab_ref, alog_ref, dtb_ref, normw_ref, o_ref, state_ref):
    c = pl.program_id(1)

    @pl.when(c == 0)
    def _():
        state_ref[...] = jnp.zeros(state_ref.shape, F32)

    ab = ab_ref[...]
    g_all = -jnp.exp(alog_ref[...]) * jax.nn.softplus(ab + dtb_ref[...])
    beta_all = jax.nn.sigmoid(ab)
    row = lax.broadcasted_iota(jnp.int32, (CHUNK, CHUNK), 0)
    colm = lax.broadcasted_iota(jnp.int32, (CHUNK, CHUNK), 1)
    tril = row >= colm
    strict = row > colm
    ltri = tril.astype(F32)
    eye = (row == colm).astype(F32)
    normw = normw_ref[...]
    d = DELTA_HEAD_DIM

    items = [(ci, h) for ci in range(GDN_CHUNKS) for h in range(N_DELTA_HEADS)]
    idx = range(len(items))

    def rows(ci):
        return slice(ci * CHUNK, (ci + 1) * CHUNK)

    g_cum = [jnp.dot(ltri, g_all[rows(ci)], precision=lax.Precision.HIGHEST,
                     preferred_element_type=F32) for ci in range(GDN_CHUNKS)]
    g_cum_t = [lax.dot_general(g_all[rows(ci)], ltri, (((0,), (1,)), ((), ())),
                               precision=lax.Precision.HIGHEST, preferred_element_type=F32)
               for ci in range(GDN_CHUNKS)]

    q_bf = [x_ref[rows(ci), h * d:(h + 1) * d] for ci, h in items]
    k_bf = [x_ref[rows(ci), DELTA_WIDTH + h * d:DELTA_WIDTH + (h + 1) * d] for ci, h in items]
    q = [q_bf[i].astype(F32) for i in idx]
    k = [k_bf[i].astype(F32) for i in idx]
    v = [x_ref[rows(ci), 2 * DELTA_WIDTH + h * d:2 * DELTA_WIDTH + (h + 1) * d].astype(F32)
         for ci, h in items]
    beta = [beta_all[rows(ci), N_DELTA_HEADS + h:N_DELTA_HEADS + h + 1] for ci, h in items]
    gc = [g_cum[ci][:, h:h + 1] for ci, h in items]
    g_last = [gc[i][CHUNK - 1:CHUNK, :] for i in idx]
    exp_g = [jnp.exp(gc[i]) for i in idx]
    decay = [jnp.exp(jnp.where(tril, gc[i] - g_cum_t[ci][h:h + 1, :], -jnp.inf))
             for i, (ci, h) in enumerate(items)]

    kb = [k[i] * beta[i] for i in idx]
    kq = [_dot_nt(jnp.concatenate([kb[i].astype(BF16), q_bf[i]], axis=0), k_bf[i]) for i in idx]
    attn = [(kq[i][CHUNK:2 * CHUNK] * decay[i]).astype(BF16) for i in idx]

    groups = [list(range(s, s + GDN_GROUP)) for s in range(0, len(items), GDN_GROUP)]
    gidx = range(len(groups))
    wide = (CHUNK, GDN_GROUP * CHUNK)
    lane_blk = lax.broadcasted_iota(jnp.int32, wide, 1) // CHUNK
    eye_cat = (lax.broadcasted_iota(jnp.int32, wide, 0)
               == lax.broadcasted_iota(jnp.int32, wide, 1) % CHUNK).astype(F32)

    def block_diag(n_cat):
        return jnp.concatenate([jnp.where(lane_blk == p, n_cat, 0.0) for p in range(GDN_GROUP)],
                               axis=0).astype(BF16)

    nj = [jnp.concatenate([jnp.where(strict, -(kq[i][0:CHUNK] * decay[i]), 0.0) for i in grp], axis=1)
          for grp in groups]
    pj = [eye_cat + nj[g] for g in gidx]
    nj = [_dot(nj[g].astype(BF16), block_diag(nj[g])) for g in gidx]
    for _ in range(4):
        both = [_dot(jnp.concatenate([pj[g], nj[g]], axis=0).astype(BF16), block_diag(nj[g]))
                for g in gidx]
        pj = [pj[g] + both[g][0:CHUNK] for g in gidx]
        nj = [both[g][CHUNK:2 * CHUNK] for g in gidx]
    pj = [pj[g] + _dot(pj[g].astype(BF16), block_diag(nj[g])) for g in gidx]
    pj = [pj[g][:, p * CHUNK:(p + 1) * CHUNK] for g in gidx for p in range(GDN_GROUP)]

    sol = [_dot(pj[i].astype(BF16),
                jnp.concatenate([v[i] * beta[i], kb[i] * exp_g[i]], axis=1).astype(BF16))
           for i in idx]
    wq = [jnp.concatenate([sol[i][:, d:2 * d], q[i] * exp_g[i]], axis=0).astype(BF16) for i in idx]
    k_dec = [(k[i] * jnp.exp(g_last[i] - gc[i])).astype(BF16) for i in idx]

    state = [state_ref[h] for h in range(N_DELTA_HEADS)]
    heads = range(N_DELTA_HEADS)
    o = []
    for ci in range(GDN_CHUNKS):
        it = [ci * N_DELTA_HEADS + h for h in heads]
        ws = [_dot(wq[i], state[h].astype(BF16)) for h, i in enumerate(it)]
        v_new = [(sol[i][:, 0:d] - ws[h][0:CHUNK]).astype(BF16) for h, i in enumerate(it)]
        o += [ws[h][CHUNK:2 * CHUNK] + _dot(attn[i], v_new[h]) for h, i in enumerate(it)]
        state = [state[h] * jnp.exp(g_last[i]) + _dot_tn(k_dec[i], v_new[h]) for h, i in enumerate(it)]
    for h in heads:
        state_ref[h] = state[h]

    for i, (ci, h) in enumerate(items):
        oh = o[i] * lax.rsqrt(jnp.mean(o[i] * o[i], axis=-1, keepdims=True) + RMS_EPS) * normw
        gate = gate_ref[rows(ci), h * d:(h + 1) * d].astype(F32)
        o_ref[rows(ci), h * d:(h + 1) * d] = (oh * gate).astype(o_ref.dtype)


def _gdn(proj, ab, alog_row, dtb_row, normw_row, batch, seq):
    ns = seq // GDN_TOK

    def cur(col):
        return lambda b, c: (b * ns + c, col)

    const = lambda b, c: (0, 0)
    return pl.pallas_call(
        _gdn_kernel,
        out_shape=jax.ShapeDtypeStruct((batch * seq, DELTA_WIDTH), BF16),
        grid=(batch, ns),
        in_specs=[
            pl.BlockSpec((GDN_TOK, DELTA_QKV), cur(COL_DQKV // DELTA_QKV)),
            pl.BlockSpec((GDN_TOK, DELTA_WIDTH), cur(COL_Z // DELTA_WIDTH)),
            pl.BlockSpec((GDN_TOK, LANES), cur(0)),
            pl.BlockSpec((1, LANES), const),
            pl.BlockSpec((1, LANES), const),
            pl.BlockSpec((1, DELTA_HEAD_DIM), const),
        ],
        out_specs=pl.BlockSpec((GDN_TOK, DELTA_WIDTH), lambda b, c: (b * ns + c, 0)),
        scratch_shapes=[
            pltpu.VMEM((N_DELTA_HEADS, DELTA_HEAD_DIM, DELTA_HEAD_DIM), F32),
        ],
        compiler_params=pltpu.CompilerParams(
            dimension_semantics=("arbitrary", "arbitrary"),
            vmem_limit_bytes=VMEM_LIMIT_BYTES),
        name="gdn",
    )(proj, proj, ab, alog_row, dtb_row, normw_row)


OUT_TM = 512


def _out_ln1_kernel(x_ref, attn_ref, delta_ref, wo_ref, g_ref, b_ref, o_ref, ob_ref):
    mixed = (_dot(attn_ref[...], wo_ref[0:ATTN_Q, :])
             + _dot(delta_ref[...], wo_ref[ATTN_Q:ATTN_Q + DELTA_WIDTH, :]))
    y = _layer_norm(DN_ALPHA * x_ref[...] + mixed, g_ref[...], b_ref[...])
    o_ref[...] = y
    ob_ref[...] = y.astype(BF16)


def _out_ln1(x2, attn_out, delta_out, w_o, g, b):
    t = x2.shape[0]
    const = lambda i: (0, 0)
    tile = lambda i: (i, 0)
    return pl.pallas_call(
        _out_ln1_kernel,
        out_shape=(jax.ShapeDtypeStruct((t, D_MODEL), F32),
                   jax.ShapeDtypeStruct((t, D_MODEL), BF16)),
        grid=(t // OUT_TM,),
        in_specs=[
            pl.BlockSpec((OUT_TM, D_MODEL), tile),
            pl.BlockSpec((OUT_TM, ATTN_Q), tile),
            pl.BlockSpec((OUT_TM, DELTA_WIDTH), tile),
            pl.BlockSpec((ATTN_Q + DELTA_WIDTH, D_MODEL), const),
            pl.BlockSpec((1, D_MODEL), const),
            pl.BlockSpec((1, D_MODEL), const),
        ],
        out_specs=(pl.BlockSpec((OUT_TM, D_MODEL), tile),
                   pl.BlockSpec((OUT_TM, D_MODEL), tile)),
        compiler_params=pltpu.CompilerParams(
            dimension_semantics=("arbitrary",),
            vmem_limit_bytes=VMEM_LIMIT_BYTES),
        name="out_ln1",
    )(x2, attn_out, delta_out, w_o, g, b)


MLP_TM = 1024
MLP_TF = 1024
MLP_STEPS = D_FF // MLP_TF
MLP_RES_ROWS = MLP_TM // MLP_STEPS


def _mlp_ln2_kernel(xb_ref, xres_ref, wup_ref, wdn_ref, g_ref, b_ref, o_ref):
    j = pl.program_id(1)

    def accumulate(first):
        a = jnp.maximum(_dot(xb_ref[...], wup_ref[...]), 0.0)
        upd = _dot((a * a).astype(BF16), wdn_ref[...])
        if first:
            o_ref[...] = upd
        else:
            o_ref[...] += upd

    pl.when(j == 0)(functools.partial(accumulate, True))
    pl.when(j > 0)(functools.partial(accumulate, False))

    rows = pl.ds(pl.multiple_of(j * MLP_RES_ROWS, MLP_RES_ROWS), MLP_RES_ROWS)
    o_ref[rows, :] += DN_ALPHA * xres_ref[...]

    @pl.when(j == MLP_STEPS - 1)
    def _():
        o_ref[...] = _layer_norm(o_ref[...], g_ref[...], b_ref[...])


def _mlp_ln2(x1, x1_bf, w_up, w_down, g, b):
    t = x1.shape[0]
    const = lambda i, j: (0, 0)
    return pl.pallas_call(
        _mlp_ln2_kernel,
        out_shape=jax.ShapeDtypeStruct((t, D_MODEL), F32),
        grid=(t // MLP_TM, MLP_STEPS),
        in_specs=[
            pl.BlockSpec((MLP_TM, D_MODEL), lambda i, j: (i, 0)),
            pl.BlockSpec((MLP_RES_ROWS, D_MODEL), lambda i, j: (i * MLP_STEPS + j, 0)),
            pl.BlockSpec((D_MODEL, MLP_TF), lambda i, j: (0, j)),
            pl.BlockSpec((MLP_TF, D_MODEL), lambda i, j: (j, 0)),
            pl.BlockSpec((1, D_MODEL), const),
            pl.BlockSpec((1, D_MODEL), const),
        ],
        out_specs=pl.BlockSpec((MLP_TM, D_MODEL), lambda i, j: (i, 0)),
        compiler_params=pltpu.CompilerParams(
            dimension_semantics=("arbitrary", "arbitrary"),
            vmem_limit_bytes=VMEM_LIMIT_BYTES),
        name="mlp_ln2",
    )(x1_bf, x1, w_up, w_down, g, b)


def _split_w_in(w_in):
    w_main = w_in[:, :SRC_AB].astype(BF16)
    w_z = w_in[:, SRC_Z:].astype(BF16)
    w_ab = jnp.pad(w_in[:, SRC_AB:SRC_Z], ((0, 0), (0, MXU_COLS - 2 * N_DELTA_HEADS))).astype(BF16)
    return w_main, w_z, w_ab


def _lane_row(v):
    return jnp.zeros((1, LANES), F32).at[0, :v.shape[0]].set(v.astype(F32))


def kernel(x, w_in, conv_w, a_log, dt_bias, delta_norm_w, attn_sinks, rel_bias, w_o, ln1_g, ln1_b,
           w_up, w_down, ln2_g, ln2_b):
    batch, seq, d = x.shape
    assert d == D_MODEL and seq % ATTN_BLOCK == 0 and seq % IN_TM == 0 and seq % GDN_TOK == 0
    assert w_in.shape[0] == DEPTH
    x2 = x.reshape(batch * seq, d)
    for l in range(DEPTH):
        proj, ab = _in_proj(x2, *_split_w_in(w_in[l]),
                            conv_w[l].reshape(CONV_WIDTH, DELTA_QKV).astype(F32), seq)
        attn_out = _swa_attention(proj, rel_bias.astype(F32), attn_sinks[l].astype(F32), batch, seq)
        delta_out = _gdn(proj, ab, _lane_row(a_log[l]), _lane_row(dt_bias[l]),
                         delta_norm_w[l].reshape(1, DELTA_HEAD_DIM).astype(F32), batch, seq)
        x1, x1_bf = _out_ln1(x2, attn_out, delta_out, w_o[l].astype(BF16),
                             ln1_g[l].reshape(1, d), ln1_b[l].reshape(1, d))
        x2 = _mlp_ln2(x1, x1_bf, w_up[l].astype(BF16), w_down[l].astype(BF16),
                      ln2_g[l].reshape(1, d), ln2_b[l].reshape(1, d))
    return x2.reshape(batch, seq, d)
```

```python
import functools
import math

import jax
import jax.numpy as jnp
import numpy as np
from jax import lax
from jax.experimental import pallas as pl
from jax.experimental.pallas import tpu as pltpu

F32 = jnp.float32
BF16 = jnp.bfloat16

D_MODEL = 2048
ATTN_HEAD_DIM = 64
N_ATTN_HEADS = 16
N_KV_HEADS = 4
ATTN_BLOCK = 128
WINDOW = 128
NEG_INF = -1e30
N_BUCKETS = 32
MAX_DISTANCE = 128
DELTA_HEAD_DIM = 128
N_DELTA_HEADS = 8
DELTA_WIDTH = N_DELTA_HEADS * DELTA_HEAD_DIM
CONV_WIDTH = 4
CHUNK = 64
D_FF = 4 * D_MODEL
DEPTH = 1
DN_ALPHA = (2.0 * DEPTH) ** 0.25
LN_EPS = 1e-5
RMS_EPS = 1e-6

ATTN_Q = N_ATTN_HEADS * ATTN_HEAD_DIM
ATTN_KV = N_KV_HEADS * ATTN_HEAD_DIM
DELTA_QKV = 3 * DELTA_WIDTH

LANES = 128
SUBLANES = 8
VMEM_LIMIT_BYTES = 56 * 1024 * 1024

COL_DQKV = 0
COL_Z = COL_DQKV + DELTA_QKV
COL_Q = COL_Z + DELTA_WIDTH
COL_K = COL_Q + ATTN_Q
COL_V = COL_K + ATTN_KV
PROJ_COLS = COL_V + ATTN_KV

SRC_Q = 0
SRC_DQKV = ATTN_Q + 2 * ATTN_KV
SRC_AB = SRC_DQKV + DELTA_QKV
SRC_Z = SRC_AB + 2 * N_DELTA_HEADS
MXU_COLS = 256


def _dot(a, b):
    return jnp.dot(a, b, preferred_element_type=F32)


def _dot_nt(a, b):
    return lax.dot_general(a, b, (((1,), (1,)), ((), ())), preferred_element_type=F32)


def _dot_tn(a, b):
    return lax.dot_general(a, b, (((0,), (0,)), ((), ())), preferred_element_type=F32)


def _layer_norm(y, g, b):
    mu = jnp.mean(y, axis=-1, keepdims=True)
    yc = y - mu
    var = jnp.mean(yc * yc, axis=-1, keepdims=True)
    return yc * lax.rsqrt(var + LN_EPS) * g + b


IN_TM = 512
IN_CHUNK = 256
IN_ROWS = 64


def _shift_rows(t, s):
    tiles = t.reshape(t.shape[0] // SUBLANES, SUBLANES, t.shape[1])
    rot = pltpu.roll(tiles, s, axis=1)
    sub = lax.broadcasted_iota(jnp.int32, tiles.shape, 1)
    return jnp.where(sub < s, pltpu.roll(rot, 1, axis=0), rot).reshape(t.shape)


def _in_proj_kernel(x_ref, wmain_ref, wz_ref, wab_ref, convw_ref, o_ref, ab_ref, tail_ref, *,
                    seq_tiles):
    i = pl.program_id(0)
    seq_start = (i % seq_tiles) == 0
    d = DELTA_HEAD_DIM

    @pl.when(i == 0)
    def _():
        tail_ref[...] = jnp.zeros(tail_ref.shape, F32)

    xb = x_ref[...].astype(BF16)

    def delta_qkv(c):
        def finish(r):
            prev = jnp.where(seq_start, 0.0, tail_ref[:, c:c + IN_CHUNK])
            tail_ref[:, c:c + IN_CHUNK] = r[IN_TM - SUBLANES:, :]
            w = [convw_ref[j:j + 1, c:c + IN_CHUNK] for j in range(CONV_WIDTH)]
            for rb in range(0, IN_TM, IN_ROWS):
                head = prev if rb == 0 else r[rb - SUBLANES:rb, :]
                xe = jnp.concatenate([head, r[rb:rb + IN_ROWS, :]], axis=0)
                xe2 = _shift_rows(xe, 2)
                y = (w[3] * xe + w[1] * xe2 + _shift_rows(w[2] * xe + w[0] * xe2, 1))[SUBLANES:, :]
                y = y * jax.nn.sigmoid(y)
                if c < 2 * DELTA_WIDTH:
                    scale = d ** -0.5 if c < DELTA_WIDTH else None
                    parts = []
                    for h in range(IN_CHUNK // d):
                        t = y[:, h * d:(h + 1) * d]
                        t = t * lax.rsqrt(jnp.sum(t * t, axis=-1, keepdims=True) + RMS_EPS)
                        parts.append(t * scale if scale is not None else t)
                    y = jnp.concatenate(parts, axis=1)
                o_ref[rb:rb + IN_ROWS, COL_DQKV + c:COL_DQKV + c + IN_CHUNK] = y.astype(o_ref.dtype)
        return wmain_ref, SRC_DQKV + c, IN_CHUNK, finish

    def gate(c):
        def finish(r):
            o_ref[:, COL_Z + c:COL_Z + c + IN_CHUNK] = (r * jax.nn.sigmoid(r)).astype(o_ref.dtype)
        return wz_ref, c, IN_CHUNK, finish

    def attn_qkv(c):
        def finish(r):
            o_ref[:, COL_Q + c:COL_Q + c + IN_CHUNK] = r.astype(o_ref.dtype)
        return wmain_ref, SRC_Q + c, IN_CHUNK, finish

    def gate_logits():
        def finish(r):
            ab_ref[...] = r[:, 0:LANES]
        return wab_ref, 0, MXU_COLS, finish

    heavy = [delta_qkv(c) for c in range(0, DELTA_QKV, IN_CHUNK)]
    light = ([attn_qkv(c) for c in range(0, ATTN_Q + 2 * ATTN_KV, IN_CHUNK)]
             + [gate(c) for c in range(0, DELTA_WIDTH, IN_CHUNK)] + [gate_logits()])
    order = [t for pair in zip(heavy, light) for t in pair]
    order += heavy[len(light):] + light[len(heavy):]
    pending = None
    for task in order:
        w_ref, col, width, finish = task
        r = _dot(xb, w_ref[:, col:col + width])
        if pending is not None:
            pending[0](pending[1])
        pending = (finish, r)
    pending[0](pending[1])


def _in_proj(x2, w_main, w_z, w_ab, conv_w, seq):
    t = x2.shape[0]
    const = lambda i: (0, 0)
    resident = pl.Buffered(1)
    return pl.pallas_call(
        functools.partial(_in_proj_kernel, seq_tiles=seq // IN_TM),
        out_shape=(jax.ShapeDtypeStruct((t, PROJ_COLS), BF16),
                   jax.ShapeDtypeStruct((t, LANES), F32)),
        grid=(t // IN_TM,),
        in_specs=[
            pl.BlockSpec((IN_TM, D_MODEL), lambda i: (i, 0)),
            pl.BlockSpec(w_main.shape, const, pipeline_mode=resident),
            pl.BlockSpec(w_z.shape, const, pipeline_mode=resident),
            pl.BlockSpec(w_ab.shape, const, pipeline_mode=resident),
            pl.BlockSpec((CONV_WIDTH, DELTA_QKV), const),
        ],
        out_specs=(pl.BlockSpec((IN_TM, PROJ_COLS), lambda i: (i, 0)),
                   pl.BlockSpec((IN_TM, LANES), lambda i: (i, 0))),
        scratch_shapes=[pltpu.VMEM((SUBLANES, DELTA_QKV), F32)],
        compiler_params=pltpu.CompilerParams(
            dimension_semantics=("arbitrary",),
            vmem_limit_bytes=VMEM_LIMIT_BYTES),
        name="in_proj",
    )(x2, w_main, w_z, w_ab, conv_w)


ATTN_QB = 4


def _t5_bucket_table():
    qi = np.arange(ATTN_BLOCK, dtype=np.int64)[:, None]
    kj = np.arange(ATTN_BLOCK, dtype=np.int64)[None, :]
    dist = np.where(kj <= qi, qi - kj, qi + ATTN_BLOCK - kj)
    assert dist.min() >= 0 and dist.max() < WINDOW
    max_exact = N_BUCKETS // 2
    nf = np.maximum(dist, 1).astype(np.float64)
    large = max_exact + (np.log(nf / max_exact) / math.log(MAX_DISTANCE / max_exact)
                         * (N_BUCKETS - max_exact)).astype(np.int64)
    large = np.minimum(large, N_BUCKETS - 1)
    return np.where(dist < max_exact, dist, large).astype(np.int32)


def _attn_kernel(bucket_ref, relb_ref, sink_ref, q_ref, kp_ref, kc_ref, vp_ref, vc_ref,
                 o_ref, bias_ref):
    b = pl.program_id(0)
    n = pl.program_id(1)
    row = lax.broadcasted_iota(jnp.int32, (ATTN_BLOCK, ATTN_BLOCK), 0)
    col = lax.broadcasted_iota(jnp.int32, (ATTN_BLOCK, ATTN_BLOCK), 1)
    in_cur = col <= row

    @pl.when((b == 0) & (n == 0))
    def _():
        bucket = bucket_ref[...]

        def head_body(h, carry):
            acc = jnp.zeros(bucket.shape, F32)
            for bk in range(N_BUCKETS):
                acc = jnp.where(bucket == bk, relb_ref[bk, h], acc)
            bias_ref[0, h] = acc
            bias_ref[1, h] = jnp.where(in_cur, acc, NEG_INF)
            return carry

        lax.fori_loop(0, N_ATTN_HEADS, head_body, 0)

    first_table = jnp.where(n == 0, 1, 0)
    lane = lax.broadcasted_iota(jnp.int32, (ATTN_BLOCK, LANES), 1)
    lo = lane < ATTN_HEAD_DIM

    q = q_ref[...] * (ATTN_HEAD_DIM ** -0.5)

    def placed(block):
        block = block.astype(F32)
        out = []
        for m in range(N_KV_HEADS // 2):
            g = block[:, m * LANES:(m + 1) * LANES]
            g_sw = pltpu.roll(g, ATTN_HEAD_DIM, axis=1)
            for t in range(2):
                src, alt = (g, g_sw) if t == 0 else (g_sw, g)
                out.append((jnp.where(lo, src, 0.0).astype(BF16), jnp.where(lo, 0.0, alt).astype(BF16)))
        return out

    def blocks(prev_ref, cur_ref):
        return [placed(prev_ref[...])] + [placed(cur_ref[s * ATTN_BLOCK:(s + 1) * ATTN_BLOCK, :])
                                          for s in range(ATTN_QB)]

    k_blk, v_blk = blocks(kp_ref, kc_ref), blocks(vp_ref, vc_ref)

    items = [(s, hd) for s in range(ATTN_QB) for hd in range(N_ATTN_HEADS)]
    idx = range(len(items))
    kv_of = [hd // (N_ATTN_HEADS // N_KV_HEADS) for hd in range(N_ATTN_HEADS)]

    def window(blk, s, hd):
        return jnp.concatenate([blk[s][kv_of[hd]][hd % 2], blk[s + 1][kv_of[hd]][hd % 2]], axis=0)

    scores = [_dot_nt(q[s * ATTN_BLOCK:(s + 1) * ATTN_BLOCK, (hd // 2) * LANES:(hd // 2 + 1) * LANES],
                      window(k_blk, s, hd)) for s, hd in items]
    logits = [jnp.where(in_cur, scores[i][:, ATTN_BLOCK:], scores[i][:, :ATTN_BLOCK])
              + bias_ref[first_table if s == 0 else 0, hd]
              for i, (s, hd) in enumerate(items)]
    mx = [jnp.maximum(jnp.max(logits[i], axis=-1, keepdims=True), sink_ref[hd])
          for i, (s, hd) in enumerate(items)]
    pr = [jnp.exp(logits[i] - mx[i]) for i in idx]
    inv = [1.0 / (jnp.sum(pr[i], axis=-1, keepdims=True) + jnp.exp(sink_ref[hd] - mx[i]))
           for i, (s, hd) in enumerate(items)]
    pv = [_dot(jnp.concatenate([jnp.where(in_cur, 0.0, pr[i]), jnp.where(in_cur, pr[i], 0.0)],
                               axis=1).astype(BF16), window(v_blk, s, hd)) * inv[i]
          for i, (s, hd) in enumerate(items)]
    for s in range(ATTN_QB):
        for p in range(N_ATTN_HEADS // 2):
            i = s * N_ATTN_HEADS + 2 * p
            o_ref[s * ATTN_BLOCK:(s + 1) * ATTN_BLOCK, p * LANES:(p + 1) * LANES] = (
                pv[i] + pv[i + 1]).astype(o_ref.dtype)


def _swa_attention(proj, rel_bias, sinks, batch, seq):
    rows = ATTN_QB * ATTN_BLOCK
    ns = seq // rows
    bucket = jnp.asarray(_t5_bucket_table())
    q_blk = COL_Q // ATTN_Q
    k_blk = COL_K // ATTN_KV
    v_blk = COL_V // ATTN_KV

    def cur(col):
        return lambda b, n: (b * ns + n, col)

    def prev(col):
        return lambda b, n: (jnp.maximum((b * ns + n) * ATTN_QB - 1, 0), col)

    smem = pl.BlockSpec(memory_space=pltpu.SMEM)
    return pl.pallas_call(
        _attn_kernel,
        out_shape=jax.ShapeDtypeStruct((batch * seq, ATTN_Q), BF16),
        grid=(batch, ns),
        in_specs=[
            pl.BlockSpec((ATTN_BLOCK, ATTN_BLOCK), lambda b, n: (0, 0)),
            smem, smem,
            pl.BlockSpec((rows, ATTN_Q), cur(q_blk)),
            pl.BlockSpec((ATTN_BLOCK, ATTN_KV), prev(k_blk)),
            pl.BlockSpec((rows, ATTN_KV), cur(k_blk)),
            pl.BlockSpec((ATTN_BLOCK, ATTN_KV), prev(v_blk)),
            pl.BlockSpec((rows, ATTN_KV), cur(v_blk)),
        ],
        out_specs=pl.BlockSpec((rows, ATTN_Q), lambda b, n: (b * ns + n, 0)),
        scratch_shapes=[pltpu.VMEM((2, N_ATTN_HEADS, ATTN_BLOCK, ATTN_BLOCK), F32)],
        compiler_params=pltpu.CompilerParams(
            dimension_semantics=("arbitrary", "arbitrary"),
            vmem_limit_bytes=VMEM_LIMIT_BYTES),
        name="swa_attn",
    )(bucket, rel_bias, sinks, proj, proj, proj, proj, proj)


GDN_CHUNKS = 8
GDN_TOK = GDN_CHUNKS * CHUNK
GDN_GROUP = MXU_COLS // CHUNK


def _gdn_kernel(x_ref, gate_ref, ab_ref, alog_ref, dtb_ref, normw_ref, o_ref, state_ref):
    c = pl.program_id(1)

    @pl.when(c == 0)
    def _():
        state_ref[...] = jnp.zeros(state_ref.shape, F32)

    ab = ab_ref[...]
    g_all = -jnp.exp(alog_ref[...]) * jax.nn.softplus(ab + dtb_ref[...])
    beta_all = jax.nn.sigmoid(ab)
    row = lax.broadcasted_iota(jnp.int32, (CHUNK, CHUNK), 0)
    colm = lax.broadcasted_iota(jnp.int32, (CHUNK, CHUNK), 1)
    tril = row >= colm
    strict = row > colm
    ltri = tril.astype(F32)
    eye = (row == colm).astype(F32)
    normw = normw_ref[...]
    d = DELTA_HEAD_DIM

    items = [(ci, h) for ci in range(GDN_CHUNKS) for h in range(N_DELTA_HEADS)]
    idx = range(len(items))

    def rows(ci):
        return slice(ci * CHUNK, (ci + 1) * CHUNK)

    g_cum = [jnp.dot(ltri, g_all[rows(ci)], precision=lax.Precision.HIGHEST,
                     preferred_element_type=F32) for ci in range(GDN_CHUNKS)]
    g_cum_t = [lax.dot_general(g_all[rows(ci)], ltri, (((0,), (1,)), ((), ())),
                               precision=lax.Precision.HIGHEST, preferred_element_type=F32)
               for ci in range(GDN_CHUNKS)]

    q_bf = [x_ref[rows(ci), h * d:(h + 1) * d] for ci, h in items]
    k_bf = [x_ref[rows(ci), DELTA_WIDTH + h * d:DELTA_WIDTH + (h + 1) * d] for ci, h in items]
    q = [q_bf[i].astype(F32) for i in idx]
    k = [k_bf[i].astype(F32) for i in idx]
    v = [x_ref[rows(ci), 2 * DELTA_WIDTH + h * d:2 * DELTA_WIDTH + (h + 1) * d].astype(F32)
         for ci, h in items]
    beta = [beta_all[rows(ci), N_DELTA_HEADS + h:N_DELTA_HEADS + h + 1] for ci, h in items]
    gc = [g_cum[ci][:, h:h + 1] for ci, h in items]
    g_last = [gc[i][CHUNK - 1:CHUNK, :] for i in idx]
    exp_g = [jnp.exp(gc[i]) for i in idx]
    decay = [jnp.exp(jnp.where(tril, gc[i] - g_cum_t[ci][h:h + 1, :], -jnp.inf))
             for i, (ci, h) in enumerate(items)]

    kb = [k[i] * beta[i] for i in idx]
    kq = [_dot_nt(jnp.concatenate([kb[i].astype(BF16), q_bf[i]], axis=0), k_bf[i]) for i in idx]
    attn = [(kq[i][CHUNK:2 * CHUNK] * decay[i]).astype(BF16) for i in idx]

    groups = [list(range(s, s + GDN_GROUP)) for s in range(0, len(items), GDN_GROUP)]
    gidx = range(len(groups))
    wide = (CHUNK, GDN_GROUP * CHUNK)
    lane_blk = lax.broadcasted_iota(jnp.int32, wide, 1) // CHUNK
    eye_cat = (lax.broadcasted_iota(jnp.int32, wide, 0)
               == lax.broadcasted_iota(jnp.int32, wide, 1) % CHUNK).astype(F32)

    def block_diag(n_cat):
        return jnp.concatenate([jnp.where(lane_blk == p, n_cat, 0.0) for p in range(GDN_GROUP)],
                               axis=0).astype(BF16)

    nj = [jnp.concatenate([jnp.where(strict, -(kq[i][0:CHUNK] * decay[i]), 0.0) for i in grp], axis=1)
          for grp in groups]
    pj = [eye_cat + nj[g] for g in gidx]
    nj = [_dot(nj[g].astype(BF16), block_diag(nj[g])) for g in gidx]
    for _ in range(4):
        both = [_dot(jnp.concatenate([pj[g], nj[g]], axis=0).astype(BF16), block_diag(nj[g]))
                for g in gidx]
        pj = [pj[g] + both[g][0:CHUNK] for g in gidx]
        nj = [both[g][CHUNK:2 * CHUNK] for g in gidx]
    pj = [pj[g] + _dot(pj[g].astype(BF16), block_diag(nj[g])) for g in gidx]
    pj = [pj[g][:, p * CHUNK:(p + 1) * CHUNK] for g in gidx for p in range(GDN_GROUP)]

    sol = [_dot(pj[i].astype(BF16),
                jnp.concatenate([v[i] * beta[i], kb[i] * exp_g[i]], axis=1).astype(BF16))
           for i in idx]
    wq = [jnp.concatenate([sol[i][:, d:2 * d], q[i] * exp_g[i]], axis=0).astype(BF16) for i in idx]
    k_dec = [(k[i] * jnp.exp(g_last[i] - gc[i])).astype(BF16) for i in idx]

    state = [state_ref[h] for h in range(N_DELTA_HEADS)]
    heads = range(N_DELTA_HEADS)
    o = []
    for ci in range(GDN_CHUNKS):
        it = [ci * N_DELTA_HEADS + h for h in heads]
        ws = [_dot(wq[i], state[h].astype(BF16)) for h, i in enumerate(it)]
        v_new = [(sol[i][:, 0:d] - ws[h][0:CHUNK]).astype(BF16) for h, i in enumerate(it)]
        o += [ws[h][CHUNK:2 * CHUNK] + _dot(attn[i], v_new[h]) for h, i in enumerate(it)]
        state = [state[h] * jnp.exp(g_last[i]) + _dot_tn(k_dec[i], v_new[h]) for h, i in enumerate(it)]
    for h in heads:
        state_ref[h] = state[h]

    for i, (ci, h) in enumerate(items):
        oh = o[i] * lax.rsqrt(jnp.mean(o[i] * o[i], axis=-1, keepdims=True) + RMS_EPS) * normw
        gate = gate_ref[rows(ci), h * d:(h + 1) * d].astype(F32)
        o_ref[rows(ci), h * d:(h + 1) * d] = (oh * gate).astype(o_ref.dtype)


def _gdn(proj, ab, alog_row, dtb_row, normw_row, batch, seq):
    ns = seq // GDN_TOK

    def cur(col):
        return lambda b, c: (b * ns + c, col)

    const = lambda b, c: (0, 0)
    return pl.pallas_call(
        _gdn_kernel,
        out_shape=jax.ShapeDtypeStruct((batch * seq, DELTA_WIDTH), BF16),
        grid=(batch, ns),
        in_specs=[
            pl.BlockSpec((GDN_TOK, DELTA_QKV), cur(COL_DQKV // DELTA_QKV)),
            pl.BlockSpec((GDN_TOK, DELTA_WIDTH), cur(COL_Z // DELTA_WIDTH)),
            pl.BlockSpec((GDN_TOK, LANES), cur(0)),
            pl.BlockSpec((1, LANES), const),
            pl.BlockSpec((1, LANES), const),
            pl.BlockSpec((1, DELTA_HEAD_DIM), const),
        ],
        out_specs=pl.BlockSpec((GDN_TOK, DELTA_WIDTH), lambda b, c: (b * ns + c, 0)),
        scratch_shapes=[
            pltpu.VMEM((N_DELTA_HEADS, DELTA_HEAD_DIM, DELTA_HEAD_DIM), F32),
        ],
        compiler_params=pltpu.CompilerParams(
            dimension_semantics=("arbitrary", "arbitrary"),
            vmem_limit_bytes=VMEM_LIMIT_BYTES),
        name="gdn",
    )(proj, proj, ab, alog_row, dtb_row, normw_row)


OUT_TM = 512


def _out_ln1_kernel(x_ref, attn_ref, delta_ref, wo_ref, g_ref, b_ref, o_ref, ob_ref):
    mixed = (_dot(attn_ref[...], wo_ref[0:ATTN_Q, :])
             + _dot(delta_ref[...], wo_ref[ATTN_Q:ATTN_Q + DELTA_WIDTH, :]))
    y = _layer_norm(DN_ALPHA * x_ref[...] + mixed, g_ref[...], b_ref[...])
    o_ref[...] = y
    ob_ref[...] = y.astype(BF16)


def _out_ln1(x2, attn_out, delta_out, w_o, g, b):
    t = x2.shape[0]
    const = lambda i: (0, 0)
    tile = lambda i: (i, 0)
    return pl.pallas_call(
        _out_ln1_kernel,
        out_shape=(jax.ShapeDtypeStruct((t, D_MODEL), F32),
                   jax.ShapeDtypeStruct((t, D_MODEL), BF16)),
        grid=(t // OUT_TM,),
        in_specs=[
            pl.BlockSpec((OUT_TM, D_MODEL), tile),
            pl.BlockSpec((OUT_TM, ATTN_Q), tile),
            pl.BlockSpec((OUT_TM, DELTA_WIDTH), tile),
            pl.BlockSpec((ATTN_Q + DELTA_WIDTH, D_MODEL), const),
            pl.BlockSpec((1, D_MODEL), const),
            pl.BlockSpec((1, D_MODEL), const),
        ],
        out_specs=(pl.BlockSpec((OUT_TM, D_MODEL), tile),
                   pl.BlockSpec((OUT_TM, D_MODEL), tile)),
        compiler_params=pltpu.CompilerParams(
            dimension_semantics=("arbitrary",),
            vmem_limit_bytes=VMEM_LIMIT_BYTES),
        name="out_ln1",
    )(x2, attn_out, delta_out, w_o, g, b)


MLP_TM = 1024
MLP_TF = 1024
MLP_STEPS = D_FF // MLP_TF
MLP_RES_ROWS = MLP_TM // MLP_STEPS


def _mlp_ln2_kernel(xb_ref, xres_ref, wup_ref, wdn_ref, g_ref, b_ref, o_ref):
    j = pl.program_id(1)

    def residual():
        rows = pl.ds(pl.multiple_of(j * MLP_RES_ROWS, MLP_RES_ROWS), MLP_RES_ROWS)
        o_ref[rows, :] += DN_ALPHA * xres_ref[...]

    def update(rows):
        a = jnp.maximum(_dot(xb_ref[rows, :], wup_ref[...]), 0.0)
        return _dot((a * a).astype(BF16), wdn_ref[...])

    def first_step():
        o_ref[...] = update(slice(None))
        residual()

    def middle_step():
        residual()
        o_ref[...] += update(slice(None))

    def last_step():
        residual()
        for r in range(0, MLP_TM, MLP_TM // 2):
            rows = slice(r, r + MLP_TM // 2)
            o_ref[rows, :] = _layer_norm(o_ref[rows, :] + update(rows), g_ref[...], b_ref[...])

    pl.when(j == 0)(first_step)
    pl.when((j > 0) & (j < MLP_STEPS - 1))(middle_step)
    pl.when(j == MLP_STEPS - 1)(last_step)


def _mlp_ln2(x1, x1_bf, w_up, w_down, g, b):
    t = x1.shape[0]
    const = lambda i, j: (0, 0)
    return pl.pallas_call(
        _mlp_ln2_kernel,
        out_shape=jax.ShapeDtypeStruct((t, D_MODEL), F32),
        grid=(t // MLP_TM, MLP_STEPS),
        in_specs=[
            pl.BlockSpec((MLP_TM, D_MODEL), lambda i, j: (i, 0)),
            pl.BlockSpec((MLP_RES_ROWS, D_MODEL), lambda i, j: (i * MLP_STEPS + j, 0)),
            pl.BlockSpec((D_MODEL, MLP_TF), lambda i, j: (0, j)),
            pl.BlockSpec((MLP_TF, D_MODEL), lambda i, j: (j, 0)),
            pl.BlockSpec((1, D_MODEL), const),
            pl.BlockSpec((1, D_MODEL), const),
        ],
        out_specs=pl.BlockSpec((MLP_TM, D_MODEL), lambda i, j: (i, 0)),
        compiler_params=pltpu.CompilerParams(
            dimension_semantics=("arbitrary", "arbitrary"),
            vmem_limit_bytes=VMEM_LIMIT_BYTES),
        name="mlp_ln2",
    )(x1_bf, x1, w_up, w_down, g, b)


def _split_w_in(w_in):
    w_main = w_in[:, :SRC_AB].astype(BF16)
    w_z = w_in[:, SRC_Z:].astype(BF16)
    w_ab = jnp.pad(w_in[:, SRC_AB:SRC_Z], ((0, 0), (0, MXU_COLS - 2 * N_DELTA_HEADS))).astype(BF16)
    return w_main, w_z, w_ab


def _lane_row(v):
    return jnp.zeros((1, LANES), F32).at[0, :v.shape[0]].set(v.astype(F32))


def kernel(x, w_in, conv_w, a_log, dt_bias, delta_norm_w, attn_sinks, rel_bias, w_o, ln1_g, ln1_b,
           w_up, w_down, ln2_g, ln2_b):
    batch, seq, d = x.shape
    assert d == D_MODEL and seq % (ATTN_QB * ATTN_BLOCK) == 0 and seq % IN_TM == 0 and seq % GDN_TOK == 0
    assert w_in.shape[0] == DEPTH
    x2 = x.reshape(batch * seq, d)
    for l in range(DEPTH):
        proj, ab = _in_proj(x2, *_split_w_in(w_in[l]),
                            conv_w[l].reshape(CONV_WIDTH, DELTA_QKV).astype(F32), seq)
        attn_out = _swa_attention(proj, rel_bias.astype(F32), attn_sinks[l].astype(F32), batch, seq)
        delta_out = _gdn(proj, ab, _lane_row(a_log[l]), _lane_row(dt_bias[l]),
                         delta_norm_w[l].reshape(1, DELTA_HEAD_DIM).astype(F32), batch, seq)
        x1, x1_bf = _out_ln1(x2, attn_out, delta_out, w_o[l].astype(BF16),
                             ln1_g[l].reshape(1, d), ln1_b[l].reshape(1, d))
        x2 = _mlp_ln2(x1, x1_bf, w_up[l].astype(BF16), w_down[l].astype(BF16),
                      ln2_g[l].reshape(1, d), ln2_b[l].reshape(1, d))
    return x2.reshape(batch, seq, d)
```

```python
import functools
import math

import jax
import jax.numpy as jnp
import numpy as np
from jax import lax
from jax.experimental import pallas as pl
from jax.experimental.pallas import tpu as pltpu

F32 = jnp.float32
BF16 = jnp.bfloat16

D_MODEL = 2048
ATTN_HEAD_DIM = 64
N_ATTN_HEADS = 16
N_KV_HEADS = 4
ATTN_BLOCK = 128
WINDOW = 128
NEG_INF = -1e30
N_BUCKETS = 32
MAX_DISTANCE = 128
DELTA_HEAD_DIM = 128
N_DELTA_HEADS = 8
DELTA_WIDTH = N_DELTA_HEADS * DELTA_HEAD_DIM
CONV_WIDTH = 4
CHUNK = 64
D_FF = 4 * D_MODEL
DEPTH = 1
DN_ALPHA = (2.0 * DEPTH) ** 0.25
LN_EPS = 1e-5
RMS_EPS = 1e-6

ATTN_Q = N_ATTN_HEADS * ATTN_HEAD_DIM
ATTN_KV = N_KV_HEADS * ATTN_HEAD_DIM
DELTA_QKV = 3 * DELTA_WIDTH

LANES = 128
SUBLANES = 8
VMEM_LIMIT_BYTES = 56 * 1024 * 1024

COL_DQKV = 0
COL_Z = COL_DQKV + DELTA_QKV
COL_Q = COL_Z + DELTA_WIDTH
COL_K = COL_Q + ATTN_Q
COL_V = COL_K + ATTN_KV
PROJ_COLS = COL_V + ATTN_KV

SRC_Q = 0
SRC_DQKV = ATTN_Q + 2 * ATTN_KV
SRC_AB = SRC_DQKV + DELTA_QKV
SRC_Z = SRC_AB + 2 * N_DELTA_HEADS
MXU_COLS = 256


def _dot(a, b):
    return jnp.dot(a, b, preferred_element_type=F32)


def _dot_nt(a, b):
    return lax.dot_general(a, b, (((1,), (1,)), ((), ())), preferred_element_type=F32)


def _dot_tn(a, b):
    return lax.dot_general(a, b, (((0,), (0,)), ((), ())), preferred_element_type=F32)


def _layer_norm(y, g, b):
    mu = jnp.mean(y, axis=-1, keepdims=True)
    yc = y - mu
    var = jnp.mean(yc * yc, axis=-1, keepdims=True)
    return yc * lax.rsqrt(var + LN_EPS) * g + b


IN_TM = 512
IN_CHUNK = 256
IN_ROWS = 64


def _shift_rows(t, s):
    tiles = t.reshape(t.shape[0] // SUBLANES, SUBLANES, t.shape[1])
    rot = pltpu.roll(tiles, s, axis=1)
    sub = lax.broadcasted_iota(jnp.int32, tiles.shape, 1)
    return jnp.where(sub < s, pltpu.roll(rot, 1, axis=0), rot).reshape(t.shape)


def _in_proj_kernel(x_ref, w_ref, convw_ref, o_ref, ab_ref, tail_ref, wz_ref, wab_ref, *,
                    seq_tiles):
    i = pl.program_id(0)
    seq_start = (i % seq_tiles) == 0
    d = DELTA_HEAD_DIM

    @pl.when(i == 0)
    def _():
        tail_ref[...] = jnp.zeros(tail_ref.shape, F32)
        wz_ref[...] = w_ref[:, SRC_Z:SRC_Z + DELTA_WIDTH]
        ab_cols = lax.broadcasted_iota(jnp.int32, wab_ref.shape, 1) < 2 * N_DELTA_HEADS
        wab_ref[...] = jnp.where(ab_cols, w_ref[:, SRC_AB:SRC_AB + MXU_COLS].astype(F32),
                                 0.0).astype(BF16)

    xb = x_ref[...].astype(BF16)

    def delta_qkv(c):
        def finish(r):
            prev = jnp.where(seq_start, 0.0, tail_ref[:, c:c + IN_CHUNK])
            tail_ref[:, c:c + IN_CHUNK] = r[IN_TM - SUBLANES:, :]
            w = [convw_ref[j:j + 1, c:c + IN_CHUNK] for j in range(CONV_WIDTH)]
            for rb in range(0, IN_TM, IN_ROWS):
                head = prev if rb == 0 else r[rb - SUBLANES:rb, :]
                xe = jnp.concatenate([head, r[rb:rb + IN_ROWS, :]], axis=0)
                xe2 = _shift_rows(xe, 2)
                y = (w[3] * xe + w[1] * xe2 + _shift_rows(w[2] * xe + w[0] * xe2, 1))[SUBLANES:, :]
                y = y * jax.nn.sigmoid(y)
                if c < 2 * DELTA_WIDTH:
                    scale = d ** -0.5 if c < DELTA_WIDTH else None
                    parts = []
                    for h in range(IN_CHUNK // d):
                        t = y[:, h * d:(h + 1) * d]
                        t = t * lax.rsqrt(jnp.sum(t * t, axis=-1, keepdims=True) + RMS_EPS)
                        parts.append(t * scale if scale is not None else t)
                    y = jnp.concatenate(parts, axis=1)
                o_ref[rb:rb + IN_ROWS, COL_DQKV + c:COL_DQKV + c + IN_CHUNK] = y.astype(o_ref.dtype)
        return w_ref, SRC_DQKV + c, IN_CHUNK, finish

    def gate(c):
        def finish(r):
            o_ref[:, COL_Z + c:COL_Z + c + IN_CHUNK] = (r * jax.nn.sigmoid(r)).astype(o_ref.dtype)
        return wz_ref, c, IN_CHUNK, finish

    def attn_qkv(c):
        def finish(r):
            o_ref[:, COL_Q + c:COL_Q + c + IN_CHUNK] = r.astype(o_ref.dtype)
        return w_ref, SRC_Q + c, IN_CHUNK, finish

    def gate_logits():
        def finish(r):
            ab_ref[...] = r[:, 0:LANES]
        return wab_ref, 0, MXU_COLS, finish

    heavy = [delta_qkv(c) for c in range(0, DELTA_QKV, IN_CHUNK)]
    light = ([attn_qkv(c) for c in range(0, ATTN_Q + 2 * ATTN_KV, IN_CHUNK)]
             + [gate(c) for c in range(0, DELTA_WIDTH, IN_CHUNK)] + [gate_logits()])
    order = [t for pair in zip(heavy, light) for t in pair]
    order += heavy[len(light):] + light[len(heavy):]
    pending = None
    for task in order:
        w_ref, col, width, finish = task
        r = _dot(xb, w_ref[:, col:col + width])
        if pending is not None:
            pending[0](pending[1])
        pending = (finish, r)
    pending[0](pending[1])


def _in_proj(x2, w_in_bf, conv_w, seq):
    t = x2.shape[0]
    const = lambda i: (0, 0)
    resident = pl.Buffered(1)
    return pl.pallas_call(
        functools.partial(_in_proj_kernel, seq_tiles=seq // IN_TM),
        out_shape=(jax.ShapeDtypeStruct((t, PROJ_COLS), BF16),
                   jax.ShapeDtypeStruct((t, LANES), F32)),
        grid=(t // IN_TM,),
        in_specs=[
            pl.BlockSpec((IN_TM, D_MODEL), lambda i: (i, 0)),
            pl.BlockSpec(w_in_bf.shape, const, pipeline_mode=resident),
            pl.BlockSpec((CONV_WIDTH, DELTA_QKV), const),
        ],
        out_specs=(pl.BlockSpec((IN_TM, PROJ_COLS), lambda i: (i, 0)),
                   pl.BlockSpec((IN_TM, LANES), lambda i: (i, 0))),
        scratch_shapes=[pltpu.VMEM((SUBLANES, DELTA_QKV), F32),
                        pltpu.VMEM((D_MODEL, DELTA_WIDTH), BF16),
                        pltpu.VMEM((D_MODEL, MXU_COLS), BF16)],
        compiler_params=pltpu.CompilerParams(
            dimension_semantics=("arbitrary",),
            vmem_limit_bytes=VMEM_LIMIT_BYTES),
        name="in_proj",
    )(x2, w_in_bf, conv_w)


ATTN_QB = 4


def _t5_bucket_table():
    qi = np.arange(ATTN_BLOCK, dtype=np.int64)[:, None]
    kj = np.arange(ATTN_BLOCK, dtype=np.int64)[None, :]
    dist = np.where(kj <= qi, qi - kj, qi + ATTN_BLOCK - kj)
    assert dist.min() >= 0 and dist.max() < WINDOW
    max_exact = N_BUCKETS // 2
    nf = np.maximum(dist, 1).astype(np.float64)
    large = max_exact + (np.log(nf / max_exact) / math.log(MAX_DISTANCE / max_exact)
                         * (N_BUCKETS - max_exact)).astype(np.int64)
    large = np.minimum(large, N_BUCKETS - 1)
    return np.where(dist < max_exact, dist, large).astype(np.int32)


def _attn_kernel(bucket_ref, relb_ref, sink_ref, q_ref, kp_ref, kc_ref, vp_ref, vc_ref,
                 o_ref, bias_ref):
    b = pl.program_id(0)
    n = pl.program_id(1)
    row = lax.broadcasted_iota(jnp.int32, (ATTN_BLOCK, ATTN_BLOCK), 0)
    col = lax.broadcasted_iota(jnp.int32, (ATTN_BLOCK, ATTN_BLOCK), 1)
    in_cur = col <= row

    @pl.when((b == 0) & (n == 0))
    def _():
        bucket = bucket_ref[...]

        def head_body(h, carry):
            acc = jnp.zeros(bucket.shape, F32)
            for bk in range(N_BUCKETS):
                acc = jnp.where(bucket == bk, relb_ref[bk, h], acc)
            bias_ref[0, h] = acc
            bias_ref[1, h] = jnp.where(in_cur, acc, NEG_INF)
            return carry

        lax.fori_loop(0, N_ATTN_HEADS, head_body, 0)

    first_table = jnp.where(n == 0, 1, 0)
    lane = lax.broadcasted_iota(jnp.int32, (ATTN_BLOCK, LANES), 1)
    lo = lane < ATTN_HEAD_DIM

    q = q_ref[...] * (ATTN_HEAD_DIM ** -0.5)

    def placed(block):
        block = block.astype(F32)
        out = []
        for m in range(N_KV_HEADS // 2):
            g = block[:, m * LANES:(m + 1) * LANES]
            g_sw = pltpu.roll(g, ATTN_HEAD_DIM, axis=1)
            for t in range(2):
                src, alt = (g, g_sw) if t == 0 else (g_sw, g)
                out.append((jnp.where(lo, src, 0.0).astype(BF16), jnp.where(lo, 0.0, alt).astype(BF16)))
        return out

    def blocks(prev_ref, cur_ref):
        return [placed(prev_ref[...])] + [placed(cur_ref[s * ATTN_BLOCK:(s + 1) * ATTN_BLOCK, :])
                                          for s in range(ATTN_QB)]

    k_blk, v_blk = blocks(kp_ref, kc_ref), blocks(vp_ref, vc_ref)

    items = [(s, hd) for s in range(ATTN_QB) for hd in range(N_ATTN_HEADS)]
    idx = range(len(items))
    kv_of = [hd // (N_ATTN_HEADS // N_KV_HEADS) for hd in range(N_ATTN_HEADS)]

    def window(blk, s, hd):
        return jnp.concatenate([blk[s][kv_of[hd]][hd % 2], blk[s + 1][kv_of[hd]][hd % 2]], axis=0)

    scores = [_dot_nt(q[s * ATTN_BLOCK:(s + 1) * ATTN_BLOCK, (hd // 2) * LANES:(hd // 2 + 1) * LANES],
                      window(k_blk, s, hd)) for s, hd in items]
    logits = [jnp.where(in_cur, scores[i][:, ATTN_BLOCK:], scores[i][:, :ATTN_BLOCK])
              + bias_ref[first_table if s == 0 else 0, hd]
              for i, (s, hd) in enumerate(items)]
    mx = [jnp.maximum(jnp.max(logits[i], axis=-1, keepdims=True), sink_ref[hd])
          for i, (s, hd) in enumerate(items)]
    pr = [jnp.exp(logits[i] - mx[i]) for i in idx]
    inv = [1.0 / (jnp.sum(pr[i], axis=-1, keepdims=True) + jnp.exp(sink_ref[hd] - mx[i]))
           for i, (s, hd) in enumerate(items)]
    pv = [_dot(jnp.concatenate([jnp.where(in_cur, 0.0, pr[i]), jnp.where(in_cur, pr[i], 0.0)],
                               axis=1).astype(BF16), window(v_blk, s, hd)) * inv[i]
          for i, (s, hd) in enumerate(items)]
    for s in range(ATTN_QB):
        for p in range(N_ATTN_HEADS // 2):
            i = s * N_ATTN_HEADS + 2 * p
            o_ref[s * ATTN_BLOCK:(s + 1) * ATTN_BLOCK, p * LANES:(p + 1) * LANES] = (
                pv[i] + pv[i + 1]).astype(o_ref.dtype)


def _swa_attention(proj, rel_bias, sinks, batch, seq):
    rows = ATTN_QB * ATTN_BLOCK
    ns = seq // rows
    bucket = jnp.asarray(_t5_bucket_table())
    q_blk = COL_Q // ATTN_Q
    k_blk = COL_K // ATTN_KV
    v_blk = COL_V // ATTN_KV

    def cur(col):
        return lambda b, n: (b * ns + n, col)

    def prev(col):
        return lambda b, n: (jnp.maximum((b * ns + n) * ATTN_QB - 1, 0), col)

    smem = pl.BlockSpec(memory_space=pltpu.SMEM)
    return pl.pallas_call(
        _attn_kernel,
        out_shape=jax.ShapeDtypeStruct((batch * seq, ATTN_Q), BF16),
        grid=(batch, ns),
        in_specs=[
            pl.BlockSpec((ATTN_BLOCK, ATTN_BLOCK), lambda b, n: (0, 0)),
            smem, smem,
            pl.BlockSpec((rows, ATTN_Q), cur(q_blk)),
            pl.BlockSpec((ATTN_BLOCK, ATTN_KV), prev(k_blk)),
            pl.BlockSpec((rows, ATTN_KV), cur(k_blk)),
            pl.BlockSpec((ATTN_BLOCK, ATTN_KV), prev(v_blk)),
            pl.BlockSpec((rows, ATTN_KV), cur(v_blk)),
        ],
        out_specs=pl.BlockSpec((rows, ATTN_Q), lambda b, n: (b * ns + n, 0)),
        scratch_shapes=[pltpu.VMEM((2, N_ATTN_HEADS, ATTN_BLOCK, ATTN_BLOCK), F32)],
        compiler_params=pltpu.CompilerParams(
            dimension_semantics=("arbitrary", "arbitrary"),
            vmem_limit_bytes=VMEM_LIMIT_BYTES),
        name="swa_attn",
    )(bucket, rel_bias, sinks, proj, proj, proj, proj, proj)


GDN_CHUNKS = 8
GDN_TOK = GDN_CHUNKS * CHUNK
GDN_GROUP = MXU_COLS // CHUNK


def _gdn_kernel(x_ref, gate_ref, ab_ref, alog_ref, dtb_ref, normw_ref, o_ref, state_ref):
    c = pl.program_id(1)

    @pl.when(c == 0)
    def _():
        state_ref[...] = jnp.zeros(state_ref.shape, F32)

    ab = ab_ref[...]
    g_all = -jnp.exp(alog_ref[...]) * jax.nn.softplus(ab + dtb_ref[...])
    beta_all = jax.nn.sigmoid(ab)
    row = lax.broadcasted_iota(jnp.int32, (CHUNK, CHUNK), 0)
    colm = lax.broadcasted_iota(jnp.int32, (CHUNK, CHUNK), 1)
    tril = row >= colm
    strict = row > colm
    ltri = tril.astype(F32)
    eye = (row == colm).astype(F32)
    normw = normw_ref[...]
    d = DELTA_HEAD_DIM

    items = [(ci, h) for ci in range(GDN_CHUNKS) for h in range(N_DELTA_HEADS)]
    idx = range(len(items))

    def rows(ci):
        return slice(ci * CHUNK, (ci + 1) * CHUNK)

    g_cum = [jnp.dot(ltri, g_all[rows(ci)], precision=lax.Precision.HIGHEST,
                     preferred_element_type=F32) for ci in range(GDN_CHUNKS)]
    g_cum_t = [lax.dot_general(g_all[rows(ci)], ltri, (((0,), (1,)), ((), ())),
                               precision=lax.Precision.HIGHEST, preferred_element_type=F32)
               for ci in range(GDN_CHUNKS)]

    q_bf = [x_ref[rows(ci), h * d:(h + 1) * d] for ci, h in items]
    k_bf = [x_ref[rows(ci), DELTA_WIDTH + h * d:DELTA_WIDTH + (h + 1) * d] for ci, h in items]
    q = [q_bf[i].astype(F32) for i in idx]
    k = [k_bf[i].astype(F32) for i in idx]
    v = [x_ref[rows(ci), 2 * DELTA_WIDTH + h * d:2 * DELTA_WIDTH + (h + 1) * d].astype(F32)
         for ci, h in items]
    beta = [beta_all[rows(ci), N_DELTA_HEADS + h:N_DELTA_HEADS + h + 1] for ci, h in items]
    gc = [g_cum[ci][:, h:h + 1] for ci, h in items]
    g_last = [gc[i][CHUNK - 1:CHUNK, :] for i in idx]
    exp_g = [jnp.exp(gc[i]) for i in idx]
    decay = [jnp.exp(jnp.where(tril, gc[i] - g_cum_t[ci][h:h + 1, :], -jnp.inf))
             for i, (ci, h) in enumerate(items)]

    kb = [k[i] * beta[i] for i in idx]
    kq = [_dot_nt(jnp.concatenate([kb[i].astype(BF16), q_bf[i]], axis=0), k_bf[i]) for i in idx]
    attn = [(kq[i][CHUNK:2 * CHUNK] * decay[i]).astype(BF16) for i in idx]

    groups = [list(range(s, s + GDN_GROUP)) for s in range(0, len(items), GDN_GROUP)]
    gidx = range(len(groups))
    wide = (CHUNK, GDN_GROUP * CHUNK)
    lane_blk = lax.broadcasted_iota(jnp.int32, wide, 1) // CHUNK
    eye_cat = (lax.broadcasted_iota(jnp.int32, wide, 0)
               == lax.broadcasted_iota(jnp.int32, wide, 1) % CHUNK).astype(F32)

    def block_diag(n_cat):
        return jnp.concatenate([jnp.where(lane_blk == p, n_cat, 0.0) for p in range(GDN_GROUP)],
                               axis=0).astype(BF16)

    nj = [jnp.concatenate([jnp.where(strict, -(kq[i][0:CHUNK] * decay[i]), 0.0) for i in grp], axis=1)
          for grp in groups]
    pj = [eye_cat + nj[g] for g in gidx]
    nj = [_dot(nj[g].astype(BF16), block_diag(nj[g])) for g in gidx]
    for _ in range(4):
        both = [_dot(jnp.concatenate([pj[g], nj[g]], axis=0).astype(BF16), block_diag(nj[g]))
                for g in gidx]
        pj = [pj[g] + both[g][0:CHUNK] for g in gidx]
        nj = [both[g][CHUNK:2 * CHUNK] for g in gidx]
    pj = [pj[g] + _dot(pj[g].astype(BF16), block_diag(nj[g])) for g in gidx]
    pj = [pj[g][:, p * CHUNK:(p + 1) * CHUNK] for g in gidx for p in range(GDN_GROUP)]

    sol = [_dot(pj[i].astype(BF16),
                jnp.concatenate([v[i] * beta[i], kb[i] * exp_g[i]], axis=1).astype(BF16))
           for i in idx]
    wq = [jnp.concatenate([sol[i][:, d:2 * d], q[i] * exp_g[i]], axis=0).astype(BF16) for i in idx]
    k_dec = [(k[i] * jnp.exp(g_last[i] - gc[i])).astype(BF16) for i in idx]

    state = [state_ref[h] for h in range(N_DELTA_HEADS)]
    heads = range(N_DELTA_HEADS)
    o = []
    for ci in range(GDN_CHUNKS):
        it = [ci * N_DELTA_HEADS + h for h in heads]
        ws = [_dot(wq[i], state[h].astype(BF16)) for h, i in enumerate(it)]
        v_new = [(sol[i][:, 0:d] - ws[h][0:CHUNK]).astype(BF16) for h, i in enumerate(it)]
        o += [ws[h][CHUNK:2 * CHUNK] + _dot(attn[i], v_new[h]) for h, i in enumerate(it)]
        state = [state[h] * jnp.exp(g_last[i]) + _dot_tn(k_dec[i], v_new[h]) for h, i in enumerate(it)]
    for h in heads:
        state_ref[h] = state[h]

    for i, (ci, h) in enumerate(items):
        oh = o[i] * lax.rsqrt(jnp.mean(o[i] * o[i], axis=-1, keepdims=True) + RMS_EPS) * normw
        gate = gate_ref[rows(ci), h * d:(h + 1) * d].astype(F32)
        o_ref[rows(ci), h * d:(h + 1) * d] = (oh * gate).astype(o_ref.dtype)


def _gdn(proj, ab, alog_row, dtb_row, normw_row, batch, seq):
    ns = seq // GDN_TOK

    def cur(col):
        return lambda b, c: (b * ns + c, col)

    const = lambda b, c: (0, 0)
    return pl.pallas_call(
        _gdn_kernel,
        out_shape=jax.ShapeDtypeStruct((batch * seq, DELTA_WIDTH), BF16),
        grid=(batch, ns),
        in_specs=[
            pl.BlockSpec((GDN_TOK, DELTA_QKV), cur(COL_DQKV // DELTA_QKV)),
            pl.BlockSpec((GDN_TOK, DELTA_WIDTH), cur(COL_Z // DELTA_WIDTH)),
            pl.BlockSpec((GDN_TOK, LANES), cur(0)),
            pl.BlockSpec((1, LANES), const),
            pl.BlockSpec((1, LANES), const),
            pl.BlockSpec((1, DELTA_HEAD_DIM), const),
        ],
        out_specs=pl.BlockSpec((GDN_TOK, DELTA_WIDTH), lambda b, c: (b * ns + c, 0)),
        scratch_shapes=[
            pltpu.VMEM((N_DELTA_HEADS, DELTA_HEAD_DIM, DELTA_HEAD_DIM), F32),
        ],
        compiler_params=pltpu.CompilerParams(
            dimension_semantics=("arbitrary", "arbitrary"),
            vmem_limit_bytes=VMEM_LIMIT_BYTES),
        name="gdn",
    )(proj, proj, ab, alog_row, dtb_row, normw_row)


OUT_TM = 512


def _out_ln1_kernel(x_ref, attn_ref, delta_ref, wo_ref, g_ref, b_ref, o_ref, ob_ref):
    mixed = (_dot(attn_ref[...], wo_ref[0:ATTN_Q, :])
             + _dot(delta_ref[...], wo_ref[ATTN_Q:ATTN_Q + DELTA_WIDTH, :]))
    y = _layer_norm(DN_ALPHA * x_ref[...] + mixed, g_ref[...], b_ref[...])
    o_ref[...] = y
    ob_ref[...] = y.astype(BF16)


def _out_ln1(x2, attn_out, delta_out, w_o, g, b):
    t = x2.shape[0]
    const = lambda i: (0, 0)
    tile = lambda i: (i, 0)
    return pl.pallas_call(
        _out_ln1_kernel,
        out_shape=(jax.ShapeDtypeStruct((t, D_MODEL), F32),
                   jax.ShapeDtypeStruct((t, D_MODEL), BF16)),
        grid=(t // OUT_TM,),
        in_specs=[
            pl.BlockSpec((OUT_TM, D_MODEL), tile),
            pl.BlockSpec((OUT_TM, ATTN_Q), tile),
            pl.BlockSpec((OUT_TM, DELTA_WIDTH), tile),
            pl.BlockSpec((ATTN_Q + DELTA_WIDTH, D_MODEL), const),
            pl.BlockSpec((1, D_MODEL), const),
            pl.BlockSpec((1, D_MODEL), const),
        ],
        out_specs=(pl.BlockSpec((OUT_TM, D_MODEL), tile),
                   pl.BlockSpec((OUT_TM, D_MODEL), tile)),
        compiler_params=pltpu.CompilerParams(
            dimension_semantics=("arbitrary",),
            vmem_limit_bytes=VMEM_LIMIT_BYTES),
        name="out_ln1",
    )(x2, attn_out, delta_out, w_o, g, b)


MLP_TM = 1024
MLP_TF = 1024
MLP_STEPS = D_FF // MLP_TF
MLP_RES_ROWS = MLP_TM // MLP_STEPS


def _mlp_ln2_kernel(xb_ref, xres_ref, wup_ref, wdn_ref, g_ref, b_ref, o_ref):
    j = pl.program_id(1)

    def accumulate(first):
        a = jnp.maximum(_dot(xb_ref[...], wup_ref[...]), 0.0)
        upd = _dot((a * a).astype(BF16), wdn_ref[...])
        if first:
            o_ref[...] = upd
        else:
            o_ref[...] += upd

    pl.when(j == 0)(functools.partial(accumulate, True))
    pl.when(j > 0)(functools.partial(accumulate, False))

    rows = pl.ds(pl.multiple_of(j * MLP_RES_ROWS, MLP_RES_ROWS), MLP_RES_ROWS)
    o_ref[rows, :] += DN_ALPHA * xres_ref[...]

    @pl.when(j == MLP_STEPS - 1)
    def _():
        o_ref[...] = _layer_norm(o_ref[...], g_ref[...], b_ref[...])


def _mlp_ln2(x1, x1_bf, w_up, w_down, g, b):
    t = x1.shape[0]
    const = lambda i, j: (0, 0)
    return pl.pallas_call(
        _mlp_ln2_kernel,
        out_shape=jax.ShapeDtypeStruct((t, D_MODEL), F32),
        grid=(t // MLP_TM, MLP_STEPS),
        in_specs=[
            pl.BlockSpec((MLP_TM, D_MODEL), lambda i, j: (i, 0)),
            pl.BlockSpec((MLP_RES_ROWS, D_MODEL), lambda i, j: (i * MLP_STEPS + j, 0)),
            pl.BlockSpec((D_MODEL, MLP_TF), lambda i, j: (0, j)),
            pl.BlockSpec((MLP_TF, D_MODEL), lambda i, j: (j, 0)),
            pl.BlockSpec((1, D_MODEL), const),
            pl.BlockSpec((1, D_MODEL), const),
        ],
        out_specs=pl.BlockSpec((MLP_TM, D_MODEL), lambda i, j: (i, 0)),
        compiler_params=pltpu.CompilerParams(
            dimension_semantics=("arbitrary", "arbitrary"),
            vmem_limit_bytes=VMEM_LIMIT_BYTES),
        name="mlp_ln2",
    )(x1_bf, x1, w_up, w_down, g, b)


def _lane_row(v):
    return jnp.zeros((1, LANES), F32).at[0, :v.shape[0]].set(v.astype(F32))


def kernel(x, w_in, conv_w, a_log, dt_bias, delta_norm_w, attn_sinks, rel_bias, w_o, ln1_g, ln1_b,
           w_up, w_down, ln2_g, ln2_b):
    batch, seq, d = x.shape
    assert d == D_MODEL and seq % (ATTN_QB * ATTN_BLOCK) == 0 and seq % IN_TM == 0 and seq % GDN_TOK == 0
    assert w_in.shape[0] == DEPTH
    x2 = x.reshape(batch * seq, d)
    for l in range(DEPTH):
        proj, ab = _in_proj(x2, w_in[l].astype(BF16),
                            conv_w[l].reshape(CONV_WIDTH, DELTA_QKV).astype(F32), seq)
        attn_out = _swa_attention(proj, rel_bias.astype(F32), attn_sinks[l].astype(F32), batch, seq)
        delta_out = _gdn(proj, ab, _lane_row(a_log[l]), _lane_row(dt_bias[l]),
                         delta_norm_w[l].reshape(1, DELTA_HEAD_DIM).astype(F32), batch, seq)
        x1, x1_bf = _out_ln1(x2, attn_out, delta_out, w_o[l].astype(BF16),
                             ln1_g[l].reshape(1, d), ln1_b[l].reshape(1, d))
        x2 = _mlp_ln2(x1, x1_bf, w_up[l].astype(BF16), w_down[l].astype(BF16),
                      ln2_g[l].reshape(1, d), ln2_b[l].reshape(1, d))
    return x2.reshape(batch, seq, d)
```

```python
import functools
import math

import jax
import jax.numpy as jnp
import numpy as np
from jax import lax
from jax.experimental import pallas as pl
from jax.experimental.pallas import tpu as pltpu

F32 = jnp.float32
BF16 = jnp.bfloat16

D_MODEL = 2048
ATTN_HEAD_DIM = 64
N_ATTN_HEADS = 16
N_KV_HEADS = 4
ATTN_BLOCK = 128
WINDOW = 128
NEG_INF = -1e30
N_BUCKETS = 32
MAX_DISTANCE = 128
DELTA_HEAD_DIM = 128
N_DELTA_HEADS = 8
DELTA_WIDTH = N_DELTA_HEADS * DELTA_HEAD_DIM
CONV_WIDTH = 4
CHUNK = 64
D_FF = 4 * D_MODEL
DEPTH = 1
DN_ALPHA = (2.0 * DEPTH) ** 0.25
LN_EPS = 1e-5
RMS_EPS = 1e-6

ATTN_Q = N_ATTN_HEADS * ATTN_HEAD_DIM
ATTN_KV = N_KV_HEADS * ATTN_HEAD_DIM
DELTA_QKV = 3 * DELTA_WIDTH

LANES = 128
SUBLANES = 8
VMEM_LIMIT_BYTES = 56 * 1024 * 1024

COL_DQKV = 0
COL_Z = COL_DQKV + DELTA_QKV
COL_Q = COL_Z + DELTA_WIDTH
COL_K = COL_Q + ATTN_Q
COL_V = COL_K + ATTN_KV
PROJ_COLS = COL_V + ATTN_KV

SRC_Q = 0
SRC_DQKV = ATTN_Q + 2 * ATTN_KV
SRC_AB = SRC_DQKV + DELTA_QKV
SRC_Z = SRC_AB + 2 * N_DELTA_HEADS
MXU_COLS = 256


def _dot(a, b):
    return jnp.dot(a, b, preferred_element_type=F32)


def _dot_nt(a, b):
    return lax.dot_general(a, b, (((1,), (1,)), ((), ())), preferred_element_type=F32)


def _dot_tn(a, b):
    return lax.dot_general(a, b, (((0,), (0,)), ((), ())), preferred_element_type=F32)


def _layer_norm(y, g, b):
    mu = jnp.mean(y, axis=-1, keepdims=True)
    yc = y - mu
    var = jnp.mean(yc * yc, axis=-1, keepdims=True)
    return yc * lax.rsqrt(var + LN_EPS) * g + b


IN_TM = 512
IN_CHUNK = 1024
IN_ROWS = 64


def _shift_rows(t, s):
    tiles = t.reshape(t.shape[0] // SUBLANES, SUBLANES, t.shape[1])
    rot = pltpu.roll(tiles, s, axis=1)
    sub = lax.broadcasted_iota(jnp.int32, tiles.shape, 1)
    return jnp.where(sub < s, pltpu.roll(rot, 1, axis=0), rot).reshape(t.shape)


def _in_proj_kernel(x_ref, w_ref, convw_ref, o_ref, ab_ref, tail_ref, wz_ref, wab_ref, *,
                    seq_tiles):
    i = pl.program_id(0)
    seq_start = (i % seq_tiles) == 0
    d = DELTA_HEAD_DIM

    @pl.when(i == 0)
    def _():
        tail_ref[...] = jnp.zeros(tail_ref.shape, F32)
        wz_ref[...] = w_ref[:, SRC_Z:SRC_Z + DELTA_WIDTH]
        ab_cols = lax.broadcasted_iota(jnp.int32, wab_ref.shape, 1) < 2 * N_DELTA_HEADS
        wab_ref[...] = jnp.where(ab_cols, w_ref[:, SRC_AB:SRC_AB + MXU_COLS].astype(F32),
                                 0.0).astype(BF16)

    xb = x_ref[...].astype(BF16)

    def delta_qkv(c, n):
        def finish(r):
            prev = jnp.where(seq_start, 0.0, tail_ref[:, c:c + n])
            tail_ref[:, c:c + n] = r[IN_TM - SUBLANES:, :]
            w = [convw_ref[j:j + 1, c:c + n] for j in range(CONV_WIDTH)]
            for rb in range(0, IN_TM, IN_ROWS):
                head = prev if rb == 0 else r[rb - SUBLANES:rb, :]
                xe = jnp.concatenate([head, r[rb:rb + IN_ROWS, :]], axis=0)
                xe2 = _shift_rows(xe, 2)
                y = (w[3] * xe + w[1] * xe2 + _shift_rows(w[2] * xe + w[0] * xe2, 1))[SUBLANES:, :]
                y = y * jax.nn.sigmoid(y)
                if c < 2 * DELTA_WIDTH:
                    scale = d ** -0.5 if c < DELTA_WIDTH else None
                    parts = []
                    for h in range(n // d):
                        t = y[:, h * d:(h + 1) * d]
                        t = t * lax.rsqrt(jnp.sum(t * t, axis=-1, keepdims=True) + RMS_EPS)
                        parts.append(t * scale if scale is not None else t)
                    y = jnp.concatenate(parts, axis=1)
                o_ref[rb:rb + IN_ROWS, COL_DQKV + c:COL_DQKV + c + n] = y.astype(o_ref.dtype)
        return w_ref, SRC_DQKV + c, n, finish

    def gate(c, n):
        def finish(r):
            o_ref[:, COL_Z + c:COL_Z + c + n] = (r * jax.nn.sigmoid(r)).astype(o_ref.dtype)
        return wz_ref, c, n, finish

    def attn_qkv(c, n):
        def finish(r):
            o_ref[:, COL_Q + c:COL_Q + c + n] = r.astype(o_ref.dtype)
        return w_ref, SRC_Q + c, n, finish

    def gate_logits():
        def finish(r):
            ab_ref[...] = r[:, 0:LANES]
        return wab_ref, 0, MXU_COLS, finish

    def chunks(total):
        return [(c, min(IN_CHUNK, total - c)) for c in range(0, total, IN_CHUNK)]

    heavy = [delta_qkv(c, n) for c, n in chunks(DELTA_QKV)]
    light = ([attn_qkv(c, n) for c, n in chunks(ATTN_Q + 2 * ATTN_KV)]
             + [gate(c, n) for c, n in chunks(DELTA_WIDTH)] + [gate_logits()])
    order = [t for pair in zip(heavy, light) for t in pair]
    order += heavy[len(light):] + light[len(heavy):]
    pending = None
    for wt_ref, col, width, finish in order:
        r = _dot(xb, wt_ref[:, col:col + width])
        if pending is not None:
            pending[0](pending[1])
        pending = (finish, r)
    pending[0](pending[1])


def _in_proj(x2, w_in_bf, conv_w, seq):
    t = x2.shape[0]
    const = lambda i: (0, 0)
    resident = pl.Buffered(1)
    return pl.pallas_call(
        functools.partial(_in_proj_kernel, seq_tiles=seq // IN_TM),
        out_shape=(jax.ShapeDtypeStruct((t, PROJ_COLS), BF16),
                   jax.ShapeDtypeStruct((t, LANES), F32)),
        grid=(t // IN_TM,),
        in_specs=[
            pl.BlockSpec((IN_TM, D_MODEL), lambda i: (i, 0)),
            pl.BlockSpec(w_in_bf.shape, const, pipeline_mode=resident),
            pl.BlockSpec((CONV_WIDTH, DELTA_QKV), const),
        ],
        out_specs=(pl.BlockSpec((IN_TM, PROJ_COLS), lambda i: (i, 0)),
                   pl.BlockSpec((IN_TM, LANES), lambda i: (i, 0))),
        scratch_shapes=[pltpu.VMEM((SUBLANES, DELTA_QKV), F32),
                        pltpu.VMEM((D_MODEL, DELTA_WIDTH), BF16),
                        pltpu.VMEM((D_MODEL, MXU_COLS), BF16)],
        compiler_params=pltpu.CompilerParams(
            dimension_semantics=("arbitrary",),
            vmem_limit_bytes=VMEM_LIMIT_BYTES),
        name="in_proj",
    )(x2, w_in_bf, conv_w)


ATTN_QB = 4


def _t5_bucket_table():
    qi = np.arange(ATTN_BLOCK, dtype=np.int64)[:, None]
    kj = np.arange(ATTN_BLOCK, dtype=np.int64)[None, :]
    dist = np.where(kj <= qi, qi - kj, qi + ATTN_BLOCK - kj)
    assert dist.min() >= 0 and dist.max() < WINDOW
    max_exact = N_BUCKETS // 2
    nf = np.maximum(dist, 1).astype(np.float64)
    large = max_exact + (np.log(nf / max_exact) / math.log(MAX_DISTANCE / max_exact)
                         * (N_BUCKETS - max_exact)).astype(np.int64)
    large = np.minimum(large, N_BUCKETS - 1)
    return np.where(dist < max_exact, dist, large).astype(np.int32)


def _attn_kernel(bucket_ref, relb_ref, sink_ref, q_ref, kp_ref, kc_ref, vp_ref, vc_ref, wo_ref,
                 o_ref, wob_ref, bias_ref):
    b = pl.program_id(0)
    n = pl.program_id(1)
    wob_ref[...] = wo_ref[...].astype(BF16)
    row = lax.broadcasted_iota(jnp.int32, (ATTN_BLOCK, ATTN_BLOCK), 0)
    col = lax.broadcasted_iota(jnp.int32, (ATTN_BLOCK, ATTN_BLOCK), 1)
    in_cur = col <= row

    @pl.when((b == 0) & (n == 0))
    def _():
        bucket = bucket_ref[...]

        def head_body(h, carry):
            acc = jnp.zeros(bucket.shape, F32)
            for bk in range(N_BUCKETS):
                acc = jnp.where(bucket == bk, relb_ref[bk, h], acc)
            bias_ref[0, h] = acc
            bias_ref[1, h] = jnp.where(in_cur, acc, NEG_INF)
            return carry

        lax.fori_loop(0, N_ATTN_HEADS, head_body, 0)

    first_table = jnp.where(n == 0, 1, 0)
    lane = lax.broadcasted_iota(jnp.int32, (ATTN_BLOCK, LANES), 1)
    lo = lane < ATTN_HEAD_DIM

    q = q_ref[...] * (ATTN_HEAD_DIM ** -0.5)

    def placed(block):
        block = block.astype(F32)
        out = []
        for m in range(N_KV_HEADS // 2):
            g = block[:, m * LANES:(m + 1) * LANES]
            g_sw = pltpu.roll(g, ATTN_HEAD_DIM, axis=1)
            for t in range(2):
                src, alt = (g, g_sw) if t == 0 else (g_sw, g)
                out.append((jnp.where(lo, src, 0.0).astype(BF16), jnp.where(lo, 0.0, alt).astype(BF16)))
        return out

    def blocks(prev_ref, cur_ref):
        return [placed(prev_ref[...])] + [placed(cur_ref[s * ATTN_BLOCK:(s + 1) * ATTN_BLOCK, :])
                                          for s in range(ATTN_QB)]

    k_blk, v_blk = blocks(kp_ref, kc_ref), blocks(vp_ref, vc_ref)

    items = [(s, hd) for s in range(ATTN_QB) for hd in range(N_ATTN_HEADS)]
    idx = range(len(items))
    kv_of = [hd // (N_ATTN_HEADS // N_KV_HEADS) for hd in range(N_ATTN_HEADS)]

    def window(blk, s, hd):
        return jnp.concatenate([blk[s][kv_of[hd]][hd % 2], blk[s + 1][kv_of[hd]][hd % 2]], axis=0)

    scores = [_dot_nt(q[s * ATTN_BLOCK:(s + 1) * ATTN_BLOCK, (hd // 2) * LANES:(hd // 2 + 1) * LANES],
                      window(k_blk, s, hd)) for s, hd in items]
    logits = [jnp.where(in_cur, scores[i][:, ATTN_BLOCK:], scores[i][:, :ATTN_BLOCK])
              + bias_ref[first_table if s == 0 else 0, hd]
              for i, (s, hd) in enumerate(items)]
    mx = [jnp.maximum(jnp.max(logits[i], axis=-1, keepdims=True), sink_ref[hd])
          for i, (s, hd) in enumerate(items)]
    pr = [jnp.exp(logits[i] - mx[i]) for i in idx]
    inv = [1.0 / (jnp.sum(pr[i], axis=-1, keepdims=True) + jnp.exp(sink_ref[hd] - mx[i]))
           for i, (s, hd) in enumerate(items)]
    pv = [_dot(jnp.concatenate([jnp.where(in_cur, 0.0, pr[i]), jnp.where(in_cur, pr[i], 0.0)],
                               axis=1).astype(BF16), window(v_blk, s, hd)) * inv[i]
          for i, (s, hd) in enumerate(items)]
    for s in range(ATTN_QB):
        for p in range(N_ATTN_HEADS // 2):
            i = s * N_ATTN_HEADS + 2 * p
            o_ref[s * ATTN_BLOCK:(s + 1) * ATTN_BLOCK, p * LANES:(p + 1) * LANES] = (
                pv[i] + pv[i + 1]).astype(o_ref.dtype)


def _swa_attention(proj, rel_bias, sinks, w_o, batch, seq):
    rows = ATTN_QB * ATTN_BLOCK
    ns = seq // rows
    assert w_o.shape[0] % (batch * ns) == 0
    wo_slab = pl.BlockSpec((w_o.shape[0] // (batch * ns), w_o.shape[1]), lambda b, n: (b * ns + n, 0))
    bucket = jnp.asarray(_t5_bucket_table())
    q_blk = COL_Q // ATTN_Q
    k_blk = COL_K // ATTN_KV
    v_blk = COL_V // ATTN_KV

    def cur(col):
        return lambda b, n: (b * ns + n, col)

    def prev(col):
        return lambda b, n: (jnp.maximum((b * ns + n) * ATTN_QB - 1, 0), col)

    smem = pl.BlockSpec(memory_space=pltpu.SMEM)
    return pl.pallas_call(
        _attn_kernel,
        out_shape=(jax.ShapeDtypeStruct((batch * seq, ATTN_Q), BF16),
                   jax.ShapeDtypeStruct(w_o.shape, BF16)),
        grid=(batch, ns),
        in_specs=[
            pl.BlockSpec((ATTN_BLOCK, ATTN_BLOCK), lambda b, n: (0, 0)),
            smem, smem,
            pl.BlockSpec((rows, ATTN_Q), cur(q_blk)),
            pl.BlockSpec((ATTN_BLOCK, ATTN_KV), prev(k_blk)),
            pl.BlockSpec((rows, ATTN_KV), cur(k_blk)),
            pl.BlockSpec((ATTN_BLOCK, ATTN_KV), prev(v_blk)),
            pl.BlockSpec((rows, ATTN_KV), cur(v_blk)),
            wo_slab,
        ],
        out_specs=(pl.BlockSpec((rows, ATTN_Q), lambda b, n: (b * ns + n, 0)), wo_slab),
        scratch_shapes=[pltpu.VMEM((2, N_ATTN_HEADS, ATTN_BLOCK, ATTN_BLOCK), F32)],
        compiler_params=pltpu.CompilerParams(
            dimension_semantics=("arbitrary", "arbitrary"),
            vmem_limit_bytes=VMEM_LIMIT_BYTES),
        name="swa_attn",
    )(bucket, rel_bias, sinks, proj, proj, proj, proj, proj, w_o)


GDN_CHUNKS = 8
GDN_TOK = GDN_CHUNKS * CHUNK
GDN_GROUP = MXU_COLS // CHUNK


def _gdn_kernel(x_ref, gate_ref, ab_ref, alog_ref, dtb_ref, normw_ref, o_ref, state_ref):
    c = pl.program_id(1)

    @pl.when(c == 0)
    def _():
        state_ref[...] = jnp.zeros(state_ref.shape, F32)

    ab = ab_ref[...]
    g_all = -jnp.exp(alog_ref[...]) * jax.nn.softplus(ab + dtb_ref[...])
    beta_all = jax.nn.sigmoid(ab)
    row = lax.broadcasted_iota(jnp.int32, (CHUNK, CHUNK), 0)
    colm = lax.broadcasted_iota(jnp.int32, (CHUNK, CHUNK), 1)
    tril = row >= colm
    strict = row > colm
    ltri = tril.astype(F32)
    eye = (row == colm).astype(F32)
    normw = normw_ref[...]
    d = DELTA_HEAD_DIM

    items = [(ci, h) for ci in range(GDN_CHUNKS) for h in range(N_DELTA_HEADS)]
    idx = range(len(items))

    def rows(ci):
        return slice(ci * CHUNK, (ci + 1) * CHUNK)

    g_cum = [jnp.dot(ltri, g_all[rows(ci)], precision=lax.Precision.HIGHEST,
                     preferred_element_type=F32) for ci in range(GDN_CHUNKS)]
    g_cum_t = [lax.dot_general(g_all[rows(ci)], ltri, (((0,), (1,)), ((), ())),
                               precision=lax.Precision.HIGHEST, preferred_element_type=F32)
               for ci in range(GDN_CHUNKS)]

    q_bf = [x_ref[rows(ci), h * d:(h + 1) * d] for ci, h in items]
    k_bf = [x_ref[rows(ci), DELTA_WIDTH + h * d:DELTA_WIDTH + (h + 1) * d] for ci, h in items]
    q = [q_bf[i].astype(F32) for i in idx]
    k = [k_bf[i].astype(F32) for i in idx]
    v = [x_ref[rows(ci), 2 * DELTA_WIDTH + h * d:2 * DELTA_WIDTH + (h + 1) * d].astype(F32)
         for ci, h in items]
    beta = [beta_all[rows(ci), N_DELTA_HEADS + h:N_DELTA_HEADS + h + 1] for ci, h in items]
    gc = [g_cum[ci][:, h:h + 1] for ci, h in items]
    g_last = [gc[i][CHUNK - 1:CHUNK, :] for i in idx]
    exp_g = [jnp.exp(gc[i]) for i in idx]
    decay = [jnp.exp(jnp.where(tril, gc[i] - g_cum_t[ci][h:h + 1, :], -jnp.inf))
             for i, (ci, h) in enumerate(items)]

    kb = [k[i] * beta[i] for i in idx]
    kq = [_dot_nt(jnp.concatenate([kb[i].astype(BF16), q_bf[i]], axis=0), k_bf[i]) for i in idx]
    attn = [(kq[i][CHUNK:2 * CHUNK] * decay[i]).astype(BF16) for i in idx]

    groups = [list(range(s, s + GDN_GROUP)) for s in range(0, len(items), GDN_GROUP)]
    gidx = range(len(groups))
    wide = (CHUNK, GDN_GROUP * CHUNK)
    lane_blk = lax.broadcasted_iota(jnp.int32, wide, 1) // CHUNK
    eye_cat = (lax.broadcasted_iota(jnp.int32, wide, 0)
               == lax.broadcasted_iota(jnp.int32, wide, 1) % CHUNK).astype(F32)

    def block_diag(n_cat):
        return jnp.concatenate([jnp.where(lane_blk == p, n_cat, 0.0) for p in range(GDN_GROUP)],
                               axis=0).astype(BF16)

    nj = [jnp.concatenate([jnp.where(strict, -(kq[i][0:CHUNK] * decay[i]), 0.0) for i in grp], axis=1)
          for grp in groups]
    pj = [eye_cat + nj[g] for g in gidx]
    nj = [_dot(nj[g].astype(BF16), block_diag(nj[g])) for g in gidx]
    for _ in range(4):
        both = [_dot(jnp.concatenate([pj[g], nj[g]], axis=0).astype(BF16), block_diag(nj[g]))
                for g in gidx]
        pj = [pj[g] + both[g][0:CHUNK] for g in gidx]
        nj = [both[g][CHUNK:2 * CHUNK] for g in gidx]
    pj = [pj[g] + _dot(pj[g].astype(BF16), block_diag(nj[g])) for g in gidx]
    pj = [pj[g][:, p * CHUNK:(p + 1) * CHUNK] for g in gidx for p in range(GDN_GROUP)]

    sol = [_dot(pj[i].astype(BF16),
                jnp.concatenate([v[i] * beta[i], kb[i] * exp_g[i]], axis=1).astype(BF16))
           for i in idx]
    wq = [jnp.concatenate([sol[i][:, d:2 * d], q[i] * exp_g[i]], axis=0).astype(BF16) for i in idx]
    k_dec = [(k[i] * jnp.exp(g_last[i] - gc[i])).astype(BF16) for i in idx]

    state = [state_ref[h] for h in range(N_DELTA_HEADS)]
    heads = range(N_DELTA_HEADS)
    o = []
    for ci in range(GDN_CHUNKS):
        it = [ci * N_DELTA_HEADS + h for h in heads]
        ws = [_dot(wq[i], state[h].astype(BF16)) for h, i in enumerate(it)]
        v_new = [(sol[i][:, 0:d] - ws[h][0:CHUNK]).astype(BF16) for h, i in enumerate(it)]
        o += [ws[h][CHUNK:2 * CHUNK] + _dot(attn[i], v_new[h]) for h, i in enumerate(it)]
        state = [state[h] * jnp.exp(g_last[i]) + _dot_tn(k_dec[i], v_new[h]) for h, i in enumerate(it)]
    for h in heads:
        state_ref[h] = state[h]

    for i, (ci, h) in enumerate(items):
        oh = o[i] * lax.rsqrt(jnp.mean(o[i] * o[i], axis=-1, keepdims=True) + RMS_EPS) * normw
        gate = gate_ref[rows(ci), h * d:(h + 1) * d].astype(F32)
        o_ref[rows(ci), h * d:(h + 1) * d] = (oh * gate).astype(o_ref.dtype)


def _gdn(proj, ab, alog_row, dtb_row, normw_row, batch, seq):
    ns = seq // GDN_TOK

    def cur(col):
        return lambda b, c: (b * ns + c, col)

    const = lambda b, c: (0, 0)
    return pl.pallas_call(
        _gdn_kernel,
        out_shape=jax.ShapeDtypeStruct((batch * seq, DELTA_WIDTH), BF16),
        grid=(batch, ns),
        in_specs=[
            pl.BlockSpec((GDN_TOK, DELTA_QKV), cur(COL_DQKV // DELTA_QKV)),
            pl.BlockSpec((GDN_TOK, DELTA_WIDTH), cur(COL_Z // DELTA_WIDTH)),
            pl.BlockSpec((GDN_TOK, LANES), cur(0)),
            pl.BlockSpec((1, LANES), const),
            pl.BlockSpec((1, LANES), const),
            pl.BlockSpec((1, DELTA_HEAD_DIM), const),
        ],
        out_specs=pl.BlockSpec((GDN_TOK, DELTA_WIDTH), lambda b, c: (b * ns + c, 0)),
        scratch_shapes=[
            pltpu.VMEM((N_DELTA_HEADS, DELTA_HEAD_DIM, DELTA_HEAD_DIM), F32),
        ],
        compiler_params=pltpu.CompilerParams(
            dimension_semantics=("arbitrary", "arbitrary"),
            vmem_limit_bytes=VMEM_LIMIT_BYTES),
        name="gdn",
    )(proj, proj, ab, alog_row, dtb_row, normw_row)


OUT_TM = 512


def _out_ln1_kernel(x_ref, attn_ref, delta_ref, wo_ref, g_ref, b_ref, wup_ref, wdn_ref,
                    o_ref, ob_ref, wupb_ref, wdnb_ref):
    mixed = (_dot(attn_ref[...], wo_ref[0:ATTN_Q, :])
             + _dot(delta_ref[...], wo_ref[ATTN_Q:ATTN_Q + DELTA_WIDTH, :]))
    wupb_ref[...] = wup_ref[...].astype(BF16)
    wdnb_ref[...] = wdn_ref[...].astype(BF16)
    y = _layer_norm(DN_ALPHA * x_ref[...] + mixed, g_ref[...], b_ref[...])
    o_ref[...] = y
    ob_ref[...] = y.astype(BF16)


def _out_ln1(x2, attn_out, delta_out, w_o, g, b, w_up, w_down):
    t = x2.shape[0]
    steps = t // OUT_TM
    assert D_MODEL % steps == 0 and D_FF % steps == 0
    const = lambda i: (0, 0)
    tile = lambda i: (i, 0)
    up_slab = pl.BlockSpec((D_MODEL // steps, D_FF), tile)
    dn_slab = pl.BlockSpec((D_FF // steps, D_MODEL), tile)
    return pl.pallas_call(
        _out_ln1_kernel,
        out_shape=(jax.ShapeDtypeStruct((t, D_MODEL), F32),
                   jax.ShapeDtypeStruct((t, D_MODEL), BF16),
                   jax.ShapeDtypeStruct(w_up.shape, BF16),
                   jax.ShapeDtypeStruct(w_down.shape, BF16)),
        grid=(steps,),
        in_specs=[
            pl.BlockSpec((OUT_TM, D_MODEL), tile),
            pl.BlockSpec((OUT_TM, ATTN_Q), tile),
            pl.BlockSpec((OUT_TM, DELTA_WIDTH), tile),
            pl.BlockSpec((ATTN_Q + DELTA_WIDTH, D_MODEL), const),
            pl.BlockSpec((1, D_MODEL), const),
            pl.BlockSpec((1, D_MODEL), const),
            up_slab, dn_slab,
        ],
        out_specs=(pl.BlockSpec((OUT_TM, D_MODEL), tile),
                   pl.BlockSpec((OUT_TM, D_MODEL), tile),
                   up_slab, dn_slab),
        compiler_params=pltpu.CompilerParams(
            dimension_semantics=("arbitrary",),
            vmem_limit_bytes=VMEM_LIMIT_BYTES),
        name="out_ln1",
    )(x2, attn_out, delta_out, w_o, g, b, w_up, w_down)


MLP_TM = 1024
MLP_TF = 1024
MLP_STEPS = D_FF // MLP_TF
MLP_RES_ROWS = MLP_TM // MLP_STEPS


def _mlp_ln2_kernel(xb_ref, xres_ref, wup_ref, wdn_ref, g_ref, b_ref, o_ref):
    j = pl.program_id(1)

    def accumulate(first):
        a = jnp.maximum(_dot(xb_ref[...], wup_ref[...]), 0.0)
        upd = _dot((a * a).astype(BF16), wdn_ref[...])
        if first:
            o_ref[...] = upd
        else:
            o_ref[...] += upd

    pl.when(j == 0)(functools.partial(accumulate, True))
    pl.when(j > 0)(functools.partial(accumulate, False))

    rows = pl.ds(pl.multiple_of(j * MLP_RES_ROWS, MLP_RES_ROWS), MLP_RES_ROWS)
    o_ref[rows, :] += DN_ALPHA * xres_ref[...]

    @pl.when(j == MLP_STEPS - 1)
    def _():
        o_ref[...] = _layer_norm(o_ref[...], g_ref[...], b_ref[...])


def _mlp_ln2(x1, x1_bf, w_up, w_down, g, b):
    t = x1.shape[0]
    const = lambda i, j: (0, 0)
    return pl.pallas_call(
        _mlp_ln2_kernel,
        out_shape=jax.ShapeDtypeStruct((t, D_MODEL), F32),
        grid=(t // MLP_TM, MLP_STEPS),
        in_specs=[
            pl.BlockSpec((MLP_TM, D_MODEL), lambda i, j: (i, 0)),
            pl.BlockSpec((MLP_RES_ROWS, D_MODEL), lambda i, j: (i * MLP_STEPS + j, 0)),
            pl.BlockSpec((D_MODEL, MLP_TF), lambda i, j: (0, j)),
            pl.BlockSpec((MLP_TF, D_MODEL), lambda i, j: (j, 0)),
            pl.BlockSpec((1, D_MODEL), const),
            pl.BlockSpec((1, D_MODEL), const),
        ],
        out_specs=pl.BlockSpec((MLP_TM, D_MODEL), lambda i, j: (i, 0)),
        compiler_params=pltpu.CompilerParams(
            dimension_semantics=("arbitrary", "arbitrary"),
            vmem_limit_bytes=VMEM_LIMIT_BYTES),
        name="mlp_ln2",
    )(x1_bf, x1, w_up, w_down, g, b)


def _lane_row(v):
    return jnp.zeros((1, LANES), F32).at[0, :v.shape[0]].set(v.astype(F32))


def kernel(x, w_in, conv_w, a_log, dt_bias, delta_norm_w, attn_sinks, rel_bias, w_o, ln1_g, ln1_b,
           w_up, w_down, ln2_g, ln2_b):
    batch, seq, d = x.shape
    assert d == D_MODEL and seq % (ATTN_QB * ATTN_BLOCK) == 0 and seq % IN_TM == 0 and seq % GDN_TOK == 0
    assert w_in.shape[0] == DEPTH
    x2 = x.reshape(batch * seq, d)
    for l in range(DEPTH):
        proj, ab = _in_proj(x2, w_in[l].astype(BF16),
                            conv_w[l].reshape(CONV_WIDTH, DELTA_QKV).astype(F32), seq)
        attn_out, w_o_bf = _swa_attention(proj, rel_bias.astype(F32), attn_sinks[l].astype(F32),
                                          w_o[l], batch, seq)
        delta_out = _gdn(proj, ab, _lane_row(a_log[l]), _lane_row(dt_bias[l]),
                         delta_norm_w[l].reshape(1, DELTA_HEAD_DIM).astype(F32), batch, seq)
        x1, x1_bf, w_up_bf, w_down_bf = _out_ln1(x2, attn_out, delta_out, w_o_bf,
                                                 ln1_g[l].reshape(1, d), ln1_b[l].reshape(1, d),
                                                 w_up[l], w_down[l])
        x2 = _mlp_ln2(x1, x1_bf, w_up_bf, w_down_bf,
                      ln2_g[l].reshape(1, d), ln2_b[l].reshape(1, d))
    return x2.reshape(batch, seq, d)
```

```python
import functools
import math

import jax
import jax.numpy as jnp
import numpy as np
from jax import lax
from jax.experimental import pallas as pl
from jax.experimental.pallas import tpu as pltpu

F32 = jnp.float32
BF16 = jnp.bfloat16

D_MODEL = 2048
ATTN_HEAD_DIM = 64
N_ATTN_HEADS = 16
N_KV_HEADS = 4
ATTN_BLOCK = 128
WINDOW = 128
NEG_INF = -1e30
N_BUCKETS = 32
MAX_DISTANCE = 128
DELTA_HEAD_DIM = 128
N_DELTA_HEADS = 8
DELTA_WIDTH = N_DELTA_HEADS * DELTA_HEAD_DIM
CONV_WIDTH = 4
CHUNK = 64
D_FF = 4 * D_MODEL
DEPTH = 1
DN_ALPHA = (2.0 * DEPTH) ** 0.25
LN_EPS = 1e-5
RMS_EPS = 1e-6

ATTN_Q = N_ATTN_HEADS * ATTN_HEAD_DIM
ATTN_KV = N_KV_HEADS * ATTN_HEAD_DIM
DELTA_QKV = 3 * DELTA_WIDTH

LANES = 128
SUBLANES = 8
VMEM_LIMIT_BYTES = 56 * 1024 * 1024

COL_DQKV = 0
COL_Z = COL_DQKV + DELTA_QKV
COL_Q = COL_Z + DELTA_WIDTH
COL_K = COL_Q + ATTN_Q
COL_V = COL_K + ATTN_KV
PROJ_COLS = COL_V + ATTN_KV

SRC_Q = 0
SRC_DQKV = ATTN_Q + 2 * ATTN_KV
SRC_AB = SRC_DQKV + DELTA_QKV
SRC_Z = SRC_AB + 2 * N_DELTA_HEADS
MXU_COLS = 256


def _dot(a, b):
    return jnp.dot(a, b, preferred_element_type=F32)


def _dot_nt(a, b):
    return lax.dot_general(a, b, (((1,), (1,)), ((), ())), preferred_element_type=F32)


def _dot_tn(a, b):
    return lax.dot_general(a, b, (((0,), (0,)), ((), ())), preferred_element_type=F32)


def _layer_norm(y, g, b):
    mu = jnp.mean(y, axis=-1, keepdims=True)
    yc = y - mu
    var = jnp.mean(yc * yc, axis=-1, keepdims=True)
    return yc * lax.rsqrt(var + LN_EPS) * g + b


IN_TM = 512
IN_CHUNK = 1024
IN_ROWS = 64


def _shift_rows(t, s):
    tiles = t.reshape(t.shape[0] // SUBLANES, SUBLANES, t.shape[1])
    rot = pltpu.roll(tiles, s, axis=1)
    sub = lax.broadcasted_iota(jnp.int32, tiles.shape, 1)
    return jnp.where(sub < s, pltpu.roll(rot, 1, axis=0), rot).reshape(t.shape)


def _in_proj_kernel(x_ref, w_ref, convw_ref, o_ref, ab_ref, tail_ref, wz_ref, wab_ref, *,
                    seq_tiles):
    i = pl.program_id(0)
    seq_start = (i % seq_tiles) == 0
    d = DELTA_HEAD_DIM

    @pl.when(i == 0)
    def _():
        tail_ref[...] = jnp.zeros(tail_ref.shape, F32)
        wz_ref[...] = w_ref[:, SRC_Z:SRC_Z + DELTA_WIDTH]
        ab_cols = lax.broadcasted_iota(jnp.int32, wab_ref.shape, 1) < 2 * N_DELTA_HEADS
        wab_ref[...] = jnp.where(ab_cols, w_ref[:, SRC_AB:SRC_AB + MXU_COLS].astype(F32),
                                 0.0).astype(BF16)

    xb = x_ref[...].astype(BF16)

    def delta_qkv(c, n):
        def finish(r):
            prev = jnp.where(seq_start, 0.0, tail_ref[:, c:c + n])
            tail_ref[:, c:c + n] = r[IN_TM - SUBLANES:, :]
            w = [convw_ref[j:j + 1, c:c + n] for j in range(CONV_WIDTH)]
            for rb in range(0, IN_TM, IN_ROWS):
                head = prev if rb == 0 else r[rb - SUBLANES:rb, :]
                xe = jnp.concatenate([head, r[rb:rb + IN_ROWS, :]], axis=0)
                xe2 = _shift_rows(xe, 2)
                y = (w[3] * xe + w[1] * xe2 + _shift_rows(w[2] * xe + w[0] * xe2, 1))[SUBLANES:, :]
                y = y * jax.nn.sigmoid(y)
                if c < 2 * DELTA_WIDTH:
                    scale = d ** -0.5 if c < DELTA_WIDTH else None
                    parts = []
                    for h in range(n // d):
                        t = y[:, h * d:(h + 1) * d]
                        t = t * lax.rsqrt(jnp.sum(t * t, axis=-1, keepdims=True) + RMS_EPS)
                        parts.append(t * scale if scale is not None else t)
                    y = jnp.concatenate(parts, axis=1)
                o_ref[rb:rb + IN_ROWS, COL_DQKV + c:COL_DQKV + c + n] = y.astype(o_ref.dtype)
        return w_ref, SRC_DQKV + c, n, finish

    def gate(c, n):
        def finish(r):
            o_ref[:, COL_Z + c:COL_Z + c + n] = (r * jax.nn.sigmoid(r)).astype(o_ref.dtype)
        return wz_ref, c, n, finish

    def attn_qkv(c, n):
        def finish(r):
            o_ref[:, COL_Q + c:COL_Q + c + n] = r.astype(o_ref.dtype)
        return w_ref, SRC_Q + c, n, finish

    def gate_logits():
        def finish(r):
            ab_ref[...] = r[:, 0:LANES]
        return wab_ref, 0, MXU_COLS, finish

    def chunks(total):
        return [(c, min(IN_CHUNK, total - c)) for c in range(0, total, IN_CHUNK)]

    heavy = [delta_qkv(c, n) for c, n in chunks(DELTA_QKV)]
    light = ([attn_qkv(c, n) for c, n in chunks(ATTN_Q + 2 * ATTN_KV)]
             + [gate(c, n) for c, n in chunks(DELTA_WIDTH)] + [gate_logits()])
    order = [t for pair in zip(heavy, light) for t in pair]
    order += heavy[len(light):] + light[len(heavy):]
    pending = None
    for wt_ref, col, width, finish in order:
        r = _dot(xb, wt_ref[:, col:col + width])
        if pending is not None:
            pending[0](pending[1])
        pending = (finish, r)
    pending[0](pending[1])


def _in_proj(x2, w_in_bf, conv_w, seq):
    t = x2.shape[0]
    const = lambda i: (0, 0)
    resident = pl.Buffered(1)
    return pl.pallas_call(
        functools.partial(_in_proj_kernel, seq_tiles=seq // IN_TM),
        out_shape=(jax.ShapeDtypeStruct((t, PROJ_COLS), BF16),
                   jax.ShapeDtypeStruct((t, LANES), F32)),
        grid=(t // IN_TM,),
        in_specs=[
            pl.BlockSpec((IN_TM, D_MODEL), lambda i: (i, 0)),
            pl.BlockSpec(w_in_bf.shape, const, pipeline_mode=resident),
            pl.BlockSpec((CONV_WIDTH, DELTA_QKV), const),
        ],
        out_specs=(pl.BlockSpec((IN_TM, PROJ_COLS), lambda i: (i, 0)),
                   pl.BlockSpec((IN_TM, LANES), lambda i: (i, 0))),
        scratch_shapes=[pltpu.VMEM((SUBLANES, DELTA_QKV), F32),
                        pltpu.VMEM((D_MODEL, DELTA_WIDTH), BF16),
                        pltpu.VMEM((D_MODEL, MXU_COLS), BF16)],
        compiler_params=pltpu.CompilerParams(
            dimension_semantics=("arbitrary",),
            vmem_limit_bytes=VMEM_LIMIT_BYTES),
        name="in_proj",
    )(x2, w_in_bf, conv_w)


ATTN_QB = 4


def _t5_bucket_table():
    qi = np.arange(ATTN_BLOCK, dtype=np.int64)[:, None]
    kj = np.arange(ATTN_BLOCK, dtype=np.int64)[None, :]
    dist = np.where(kj <= qi, qi - kj, qi + ATTN_BLOCK - kj)
    assert dist.min() >= 0 and dist.max() < WINDOW
    max_exact = N_BUCKETS // 2
    nf = np.maximum(dist, 1).astype(np.float64)
    large = max_exact + (np.log(nf / max_exact) / math.log(MAX_DISTANCE / max_exact)
                         * (N_BUCKETS - max_exact)).astype(np.int64)
    large = np.minimum(large, N_BUCKETS - 1)
    return np.where(dist < max_exact, dist, large).astype(np.int32)


def _attn_kernel(bucket_ref, relb_ref, sink_ref, q_ref, kp_ref, kc_ref, vp_ref, vc_ref, wo_ref,
                 o_ref, wob_ref, bias_ref):
    b = pl.program_id(0)
    n = pl.program_id(1)
    wob_ref[...] = wo_ref[...].astype(BF16)
    row = lax.broadcasted_iota(jnp.int32, (ATTN_BLOCK, ATTN_BLOCK), 0)
    col = lax.broadcasted_iota(jnp.int32, (ATTN_BLOCK, ATTN_BLOCK), 1)
    in_cur = col <= row

    @pl.when((b == 0) & (n == 0))
    def _():
        bucket = bucket_ref[...]

        def head_body(h, carry):
            acc = jnp.zeros(bucket.shape, F32)
            for bk in range(N_BUCKETS):
                acc = jnp.where(bucket == bk, relb_ref[bk, h], acc)
            bias_ref[0, h] = acc
            bias_ref[1, h] = jnp.where(in_cur, acc, NEG_INF)
            return carry

        lax.fori_loop(0, N_ATTN_HEADS, head_body, 0)

    first_table = jnp.where(n == 0, 1, 0)
    lane = lax.broadcasted_iota(jnp.int32, (ATTN_BLOCK, LANES), 1)
    lo = lane < ATTN_HEAD_DIM

    q = q_ref[...] * (ATTN_HEAD_DIM ** -0.5)

    def placed(block):
        block = block.astype(F32)
        out = []
        for m in range(N_KV_HEADS // 2):
            g = block[:, m * LANES:(m + 1) * LANES]
            g_sw = pltpu.roll(g, ATTN_HEAD_DIM, axis=1)
            for t in range(2):
                src, alt = (g, g_sw) if t == 0 else (g_sw, g)
                out.append((jnp.where(lo, src, 0.0).astype(BF16), jnp.where(lo, 0.0, alt).astype(BF16)))
        return out

    def blocks(prev_ref, cur_ref):
        return [placed(prev_ref[...])] + [placed(cur_ref[s * ATTN_BLOCK:(s + 1) * ATTN_BLOCK, :])
                                          for s in range(ATTN_QB)]

    k_blk, v_blk = blocks(kp_ref, kc_ref), blocks(vp_ref, vc_ref)

    items = [(s, hd) for s in range(ATTN_QB) for hd in range(N_ATTN_HEADS)]
    idx = range(len(items))
    kv_of = [hd // (N_ATTN_HEADS // N_KV_HEADS) for hd in range(N_ATTN_HEADS)]

    def window(blk, s, hd):
        return jnp.concatenate([blk[s][kv_of[hd]][hd % 2], blk[s + 1][kv_of[hd]][hd % 2]], axis=0)

    scores = [_dot_nt(q[s * ATTN_BLOCK:(s + 1) * ATTN_BLOCK, (hd // 2) * LANES:(hd // 2 + 1) * LANES],
                      window(k_blk, s, hd)) for s, hd in items]
    logits = [jnp.where(in_cur, scores[i][:, ATTN_BLOCK:], scores[i][:, :ATTN_BLOCK])
              + bias_ref[first_table if s == 0 else 0, hd]
              for i, (s, hd) in enumerate(items)]
    mx = [jnp.maximum(jnp.max(logits[i], axis=-1, keepdims=True), sink_ref[hd])
          for i, (s, hd) in enumerate(items)]
    pr = [jnp.exp(logits[i] - mx[i]) for i in idx]
    inv = [1.0 / (jnp.sum(pr[i], axis=-1, keepdims=True) + jnp.exp(sink_ref[hd] - mx[i]))
           for i, (s, hd) in enumerate(items)]
    pv = [_dot(jnp.concatenate([jnp.where(in_cur, 0.0, pr[i]), jnp.where(in_cur, pr[i], 0.0)],
                               axis=1).astype(BF16), window(v_blk, s, hd)) * inv[i]
          for i, (s, hd) in enumerate(items)]
    for s in range(ATTN_QB):
        for p in range(N_ATTN_HEADS // 2):
            i = s * N_ATTN_HEADS + 2 * p
            o_ref[s * ATTN_BLOCK:(s + 1) * ATTN_BLOCK, p * LANES:(p + 1) * LANES] = (
                pv[i] + pv[i + 1]).astype(o_ref.dtype)


def _swa_attention(proj, rel_bias, sinks, w_o, batch, seq):
    rows = ATTN_QB * ATTN_BLOCK
    ns = seq // rows
    assert w_o.shape[0] % (batch * ns) == 0
    wo_slab = pl.BlockSpec((w_o.shape[0] // (batch * ns), w_o.shape[1]), lambda b, n: (b * ns + n, 0))
    bucket = jnp.asarray(_t5_bucket_table())
    q_blk = COL_Q // ATTN_Q
    k_blk = COL_K // ATTN_KV
    v_blk = COL_V // ATTN_KV

    def cur(col):
        return lambda b, n: (b * ns + n, col)

    def prev(col):
        return lambda b, n: (jnp.maximum((b * ns + n) * ATTN_QB - 1, 0), col)

    smem = pl.BlockSpec(memory_space=pltpu.SMEM)
    return pl.pallas_call(
        _attn_kernel,
        out_shape=(jax.ShapeDtypeStruct((batch * seq, ATTN_Q), BF16),
                   jax.ShapeDtypeStruct(w_o.shape, BF16)),
        grid=(batch, ns),
        in_specs=[
            pl.BlockSpec((ATTN_BLOCK, ATTN_BLOCK), lambda b, n: (0, 0)),
            smem, smem,
            pl.BlockSpec((rows, ATTN_Q), cur(q_blk)),
            pl.BlockSpec((ATTN_BLOCK, ATTN_KV), prev(k_blk)),
            pl.BlockSpec((rows, ATTN_KV), cur(k_blk)),
            pl.BlockSpec((ATTN_BLOCK, ATTN_KV), prev(v_blk)),
            pl.BlockSpec((rows, ATTN_KV), cur(v_blk)),
            wo_slab,
        ],
        out_specs=(pl.BlockSpec((rows, ATTN_Q), lambda b, n: (b * ns + n, 0)), wo_slab),
        scratch_shapes=[pltpu.VMEM((2, N_ATTN_HEADS, ATTN_BLOCK, ATTN_BLOCK), F32)],
        compiler_params=pltpu.CompilerParams(
            dimension_semantics=("arbitrary", "arbitrary"),
            vmem_limit_bytes=VMEM_LIMIT_BYTES),
        name="swa_attn",
    )(bucket, rel_bias, sinks, proj, proj, proj, proj, proj, w_o)


GDN_CHUNKS = 8
GDN_TOK = GDN_CHUNKS * CHUNK
GDN_GROUP = MXU_COLS // CHUNK


def _gdn_kernel(x_ref, gate_ref, ab_ref, alog_ref, dtb_ref, normw_ref, wup_ref, wdn_ref,
                o_ref, wupb_ref, wdnb_ref, state_ref):
    c = pl.program_id(1)
    wupb_ref[...] = wup_ref[...].astype(BF16)
    wdnb_ref[...] = wdn_ref[...].astype(BF16)

    @pl.when(c == 0)
    def _():
        state_ref[...] = jnp.zeros(state_ref.shape, F32)

    ab = ab_ref[...]
    g_all = -jnp.exp(alog_ref[...]) * jax.nn.softplus(ab + dtb_ref[...])
    beta_all = jax.nn.sigmoid(ab)
    row = lax.broadcasted_iota(jnp.int32, (CHUNK, CHUNK), 0)
    colm = lax.broadcasted_iota(jnp.int32, (CHUNK, CHUNK), 1)
    tril = row >= colm
    strict = row > colm
    ltri = tril.astype(F32)
    eye = (row == colm).astype(F32)
    normw = normw_ref[...]
    d = DELTA_HEAD_DIM

    items = [(ci, h) for ci in range(GDN_CHUNKS) for h in range(N_DELTA_HEADS)]
    idx = range(len(items))

    def rows(ci):
        return slice(ci * CHUNK, (ci + 1) * CHUNK)

    g_cum = [jnp.dot(ltri, g_all[rows(ci)], precision=lax.Precision.HIGHEST,
                     preferred_element_type=F32) for ci in range(GDN_CHUNKS)]
    g_cum_t = [lax.dot_general(g_all[rows(ci)], ltri, (((0,), (1,)), ((), ())),
                               precision=lax.Precision.HIGHEST, preferred_element_type=F32)
               for ci in range(GDN_CHUNKS)]

    q_bf = [x_ref[rows(ci), h * d:(h + 1) * d] for ci, h in items]
    k_bf = [x_ref[rows(ci), DELTA_WIDTH + h * d:DELTA_WIDTH + (h + 1) * d] for ci, h in items]
    q = [q_bf[i].astype(F32) for i in idx]
    k = [k_bf[i].astype(F32) for i in idx]
    v = [x_ref[rows(ci), 2 * DELTA_WIDTH + h * d:2 * DELTA_WIDTH + (h + 1) * d].astype(F32)
         for ci, h in items]
    beta = [beta_all[rows(ci), N_DELTA_HEADS + h:N_DELTA_HEADS + h + 1] for ci, h in items]
    gc = [g_cum[ci][:, h:h + 1] for ci, h in items]
    g_last = [gc[i][CHUNK - 1:CHUNK, :] for i in idx]
    exp_g = [jnp.exp(gc[i]) for i in idx]
    decay = [jnp.exp(jnp.where(tril, gc[i] - g_cum_t[ci][h:h + 1, :], -jnp.inf))
             for i, (ci, h) in enumerate(items)]

    kb = [k[i] * beta[i] for i in idx]
    kq = [_dot_nt(jnp.concatenate([kb[i].astype(BF16), q_bf[i]], axis=0), k_bf[i]) for i in idx]
    attn = [(kq[i][CHUNK:2 * CHUNK] * decay[i]).astype(BF16) for i in idx]

    groups = [list(range(s, s + GDN_GROUP)) for s in range(0, len(items), GDN_GROUP)]
    gidx = range(len(groups))
    wide = (CHUNK, GDN_GROUP * CHUNK)
    lane_blk = lax.broadcasted_iota(jnp.int32, wide, 1) // CHUNK
    eye_cat = (lax.broadcasted_iota(jnp.int32, wide, 0)
               == lax.broadcasted_iota(jnp.int32, wide, 1) % CHUNK).astype(F32)

    def block_diag(n_cat):
        return jnp.concatenate([jnp.where(lane_blk == p, n_cat, 0.0) for p in range(GDN_GROUP)],
                               axis=0).astype(BF16)

    nj = [jnp.concatenate([jnp.where(strict, -(kq[i][0:CHUNK] * decay[i]), 0.0) for i in grp], axis=1)
          for grp in groups]
    pj = [eye_cat + nj[g] for g in gidx]
    nj = [_dot(nj[g].astype(BF16), block_diag(nj[g])) for g in gidx]
    for _ in range(4):
        both = [_dot(jnp.concatenate([pj[g], nj[g]], axis=0).astype(BF16), block_diag(nj[g]))
                for g in gidx]
        pj = [pj[g] + both[g][0:CHUNK] for g in gidx]
        nj = [both[g][CHUNK:2 * CHUNK] for g in gidx]
    pj = [pj[g] + _dot(pj[g].astype(BF16), block_diag(nj[g])) for g in gidx]
    pj = [pj[g][:, p * CHUNK:(p + 1) * CHUNK] for g in gidx for p in range(GDN_GROUP)]

    sol = [_dot(pj[i].astype(BF16),
                jnp.concatenate([v[i] * beta[i], kb[i] * exp_g[i]], axis=1).astype(BF16))
           for i in idx]
    wq = [jnp.concatenate([sol[i][:, d:2 * d], q[i] * exp_g[i]], axis=0).astype(BF16) for i in idx]
    k_dec = [(k[i] * jnp.exp(g_last[i] - gc[i])).astype(BF16) for i in idx]

    state = [state_ref[h] for h in range(N_DELTA_HEADS)]
    heads = range(N_DELTA_HEADS)
    o = []
    for ci in range(GDN_CHUNKS):
        it = [ci * N_DELTA_HEADS + h for h in heads]
        ws = [_dot(wq[i], state[h].astype(BF16)) for h, i in enumerate(it)]
        v_new = [(sol[i][:, 0:d] - ws[h][0:CHUNK]).astype(BF16) for h, i in enumerate(it)]
        o += [ws[h][CHUNK:2 * CHUNK] + _dot(attn[i], v_new[h]) for h, i in enumerate(it)]
        state = [state[h] * jnp.exp(g_last[i]) + _dot_tn(k_dec[i], v_new[h]) for h, i in enumerate(it)]
    for h in heads:
        state_ref[h] = state[h]

    for i, (ci, h) in enumerate(items):
        oh = o[i] * lax.rsqrt(jnp.mean(o[i] * o[i], axis=-1, keepdims=True) + RMS_EPS) * normw
        gate = gate_ref[rows(ci), h * d:(h + 1) * d].astype(F32)
        o_ref[rows(ci), h * d:(h + 1) * d] = (oh * gate).astype(o_ref.dtype)


def _gdn(proj, ab, alog_row, dtb_row, normw_row, w_up, w_down, batch, seq):
    ns = seq // GDN_TOK
    steps = batch * ns
    assert w_up.shape[0] % steps == 0 and w_down.shape[0] % steps == 0

    def cur(col):
        return lambda b, c: (b * ns + c, col)

    const = lambda b, c: (0, 0)
    up_slab = pl.BlockSpec((w_up.shape[0] // steps, w_up.shape[1]), cur(0))
    dn_slab = pl.BlockSpec((w_down.shape[0] // steps, w_down.shape[1]), cur(0))
    return pl.pallas_call(
        _gdn_kernel,
        out_shape=(jax.ShapeDtypeStruct((batch * seq, DELTA_WIDTH), BF16),
                   jax.ShapeDtypeStruct(w_up.shape, BF16),
                   jax.ShapeDtypeStruct(w_down.shape, BF16)),
        grid=(batch, ns),
        in_specs=[
            pl.BlockSpec((GDN_TOK, DELTA_QKV), cur(COL_DQKV // DELTA_QKV)),
            pl.BlockSpec((GDN_TOK, DELTA_WIDTH), cur(COL_Z // DELTA_WIDTH)),
            pl.BlockSpec((GDN_TOK, LANES), cur(0)),
            pl.BlockSpec((1, LANES), const),
            pl.BlockSpec((1, LANES), const),
            pl.BlockSpec((1, DELTA_HEAD_DIM), const),
            up_slab, dn_slab,
        ],
        out_specs=(pl.BlockSpec((GDN_TOK, DELTA_WIDTH), lambda b, c: (b * ns + c, 0)),
                   up_slab, dn_slab),
        scratch_shapes=[
            pltpu.VMEM((N_DELTA_HEADS, DELTA_HEAD_DIM, DELTA_HEAD_DIM), F32),
        ],
        compiler_params=pltpu.CompilerParams(
            dimension_semantics=("arbitrary", "arbitrary"),
            vmem_limit_bytes=VMEM_LIMIT_BYTES),
        name="gdn",
    )(proj, proj, ab, alog_row, dtb_row, normw_row, w_up, w_down)


OUT_TM = 512


def _out_ln1_kernel(x_ref, attn_ref, delta_ref, wo_ref, g_ref, b_ref, o_ref, ob_ref):
    mixed = (_dot(attn_ref[...], wo_ref[0:ATTN_Q, :])
             + _dot(delta_ref[...], wo_ref[ATTN_Q:ATTN_Q + DELTA_WIDTH, :]))
    y = _layer_norm(DN_ALPHA * x_ref[...] + mixed, g_ref[...], b_ref[...])
    o_ref[...] = y
    ob_ref[...] = y.astype(BF16)


def _out_ln1(x2, attn_out, delta_out, w_o, g, b):
    t = x2.shape[0]
    const = lambda i: (0, 0)
    tile = lambda i: (i, 0)
    return pl.pallas_call(
        _out_ln1_kernel,
        out_shape=(jax.ShapeDtypeStruct((t, D_MODEL), F32),
                   jax.ShapeDtypeStruct((t, D_MODEL), BF16)),
        grid=(t // OUT_TM,),
        in_specs=[
            pl.BlockSpec((OUT_TM, D_MODEL), tile),
            pl.BlockSpec((OUT_TM, ATTN_Q), tile),
            pl.BlockSpec((OUT_TM, DELTA_WIDTH), tile),
            pl.BlockSpec((ATTN_Q + DELTA_WIDTH, D_MODEL), const),
            pl.BlockSpec((1, D_MODEL), const),
            pl.BlockSpec((1, D_MODEL), const),
        ],
        out_specs=(pl.BlockSpec((OUT_TM, D_MODEL), tile),
                   pl.BlockSpec((OUT_TM, D_MODEL), tile)),
        compiler_params=pltpu.CompilerParams(
            dimension_semantics=("arbitrary",),
            vmem_limit_bytes=VMEM_LIMIT_BYTES),
        name="out_ln1",
    )(x2, attn_out, delta_out, w_o, g, b)


MLP_TM = 1024
MLP_TF = 1024
MLP_STEPS = D_FF // MLP_TF
MLP_RES_ROWS = MLP_TM // MLP_STEPS


def _mlp_ln2_kernel(xb_ref, xres_ref, wup_ref, wdn_ref, g_ref, b_ref, o_ref):
    j = pl.program_id(1)

    def accumulate(first):
        a = jnp.maximum(_dot(xb_ref[...], wup_ref[...]), 0.0)
        upd = _dot((a * a).astype(BF16), wdn_ref[...])
        if first:
            o_ref[...] = upd
        else:
            o_ref[...] += upd

    pl.when(j == 0)(functools.partial(accumulate, True))
    pl.when(j > 0)(functools.partial(accumulate, False))

    rows = pl.ds(pl.multiple_of(j * MLP_RES_ROWS, MLP_RES_ROWS), MLP_RES_ROWS)
    o_ref[rows, :] += DN_ALPHA * xres_ref[...]

    @pl.when(j == MLP_STEPS - 1)
    def _():
        o_ref[...] = _layer_norm(o_ref[...], g_ref[...], b_ref[...])


def _mlp_ln2(x1, x1_bf, w_up, w_down, g, b):
    t = x1.shape[0]
    const = lambda i, j: (0, 0)
    return pl.pallas_call(
        _mlp_ln2_kernel,
        out_shape=jax.ShapeDtypeStruct((t, D_MODEL), F32),
        grid=(t // MLP_TM, MLP_STEPS),
        in_specs=[
            pl.BlockSpec((MLP_TM, D_MODEL), lambda i, j: (i, 0)),
            pl.BlockSpec((MLP_RES_ROWS, D_MODEL), lambda i, j: (i * MLP_STEPS + j, 0)),
            pl.BlockSpec((D_MODEL, MLP_TF), lambda i, j: (0, j)),
            pl.BlockSpec((MLP_TF, D_MODEL), lambda i, j: (j, 0)),
            pl.BlockSpec((1, D_MODEL), const),
            pl.BlockSpec((1, D_MODEL), const),
        ],
        out_specs=pl.BlockSpec((MLP_TM, D_MODEL), lambda i, j: (i, 0)),
        compiler_params=pltpu.CompilerParams(
            dimension_semantics=("arbitrary", "arbitrary"),
            vmem_limit_bytes=VMEM_LIMIT_BYTES),
        name="mlp_ln2",
    )(x1_bf, x1, w_up, w_down, g, b)


def _lane_row(v):
    return jnp.zeros((1, LANES), F32).at[0, :v.shape[0]].set(v.astype(F32))


def kernel(x, w_in, conv_w, a_log, dt_bias, delta_norm_w, attn_sinks, rel_bias, w_o, ln1_g, ln1_b,
           w_up, w_down, ln2_g, ln2_b):
    batch, seq, d = x.shape
    assert d == D_MODEL and seq % (ATTN_QB * ATTN_BLOCK) == 0 and seq % IN_TM == 0 and seq % GDN_TOK == 0
    assert w_in.shape[0] == DEPTH
    x2 = x.reshape(batch * seq, d)
    for l in range(DEPTH):
        proj, ab = _in_proj(x2, w_in[l].astype(BF16),
                            conv_w[l].reshape(CONV_WIDTH, DELTA_QKV).astype(F32), seq)
        attn_out, w_o_bf = _swa_attention(proj, rel_bias.astype(F32), attn_sinks[l].astype(F32),
                                          w_o[l], batch, seq)
        delta_out, w_up_bf, w_down_bf = _gdn(
            proj, ab, _lane_row(a_log[l]), _lane_row(dt_bias[l]),
            delta_norm_w[l].reshape(1, DELTA_HEAD_DIM).astype(F32), w_up[l], w_down[l], batch, seq)
        x1, x1_bf = _out_ln1(x2, attn_out, delta_out, w_o_bf,
                             ln1_g[l].reshape(1, d), ln1_b[l].reshape(1, d))
        x2 = _mlp_ln2(x1, x1_bf, w_up_bf, w_down_bf,
                      ln2_g[l].reshape(1, d), ln2_b[l].reshape(1, d))
    return x2.reshape(batch, seq, d)
```
